```python
import math
import jax, jax.numpy as jnp
from jax import lax
import numpy as np

D_MODEL = 1024
BATCH = 16
SEQ = 2048
DEPTH = 4

N_MIXERS = 2

MLA_HEADS = 8
Q_LORA = 512
KV_LORA = 256
QK_NOPE = 128
QK_ROPE = 64
V_DIM = 128
ROPE_THETA = 10000.0
Q_BLOCK = 128
MLA_IN_DIM = Q_LORA + KV_LORA + QK_ROPE

SSM_EXPAND = 2
SSM_INNER = SSM_EXPAND * D_MODEL
SSM_HEADDIM = 64
SSM_HEADS = SSM_INNER // SSM_HEADDIM
SSM_GROUPS = 8
SSM_HPG = SSM_HEADS // SSM_GROUPS
SSM_STATE = 128
SSM_CONV = 4
SSM_CHUNK = 128
SSM_CONV_DIM = SSM_INNER + 2 * SSM_GROUPS * SSM_STATE
SSM_IN_DIM = 2 * SSM_INNER + 2 * SSM_GROUPS * SSM_STATE + SSM_HEADS
SSM_NORM_EPS = 1e-5

FFN_HIDDEN = ((-(-8 * D_MODEL // 3) + 255) // 256) * 256

NORM_EPS = 1e-6
N_MLA = (DEPTH + N_MIXERS - 1) // N_MIXERS
N_SSM = DEPTH // N_MIXERS

kernel_name = "hybrid_mla_mamba2_interleaved"


def rms_norm(x, g, eps=NORM_EPS):
    xf = x.astype(jnp.float32)
    y = xf * lax.rsqrt(jnp.mean(xf * xf, axis=-1, keepdims=True) + eps)
    return (y * g.astype(jnp.float32)).astype(x.dtype)


def rope_tables(positions):
    half = QK_ROPE // 2
    inv_freq = jnp.power(ROPE_THETA, -jnp.arange(half, dtype=jnp.float32) / half)
    ang = positions.astype(jnp.float32)[..., None] * inv_freq
    return jnp.cos(ang), jnp.sin(ang)


def apply_rope(x, cos, sin):
    xf = x.astype(jnp.float32)
    half = x.shape[-1] // 2
    x1, x2 = xf[..., :half], xf[..., half:]
    out = jnp.concatenate([x1 * cos - x2 * sin, x2 * cos + x1 * sin], axis=-1)
    return out.astype(x.dtype)


def mla_mixer(h, cos, sin, w_in, q_norm, kv_norm, w_uq, w_ukv, w_o):
    b, s, _ = h.shape
    lat = h @ w_in
    q_lat, kv_lat, k_rope = jnp.split(lat, [Q_LORA, Q_LORA + KV_LORA], axis=-1)
    q = (rms_norm(q_lat, q_norm) @ w_uq).reshape(b, s, MLA_HEADS, QK_NOPE + QK_ROPE)
    q_nope = q[..., :QK_NOPE]
    q_rope = apply_rope(q[..., QK_NOPE:], cos[:, :, None, :], sin[:, :, None, :])
    kv = (rms_norm(kv_lat, kv_norm) @ w_ukv).reshape(b, s, MLA_HEADS, QK_NOPE + V_DIM)
    k_nope, v = kv[..., :QK_NOPE], kv[..., QK_NOPE:]
    k_rope = apply_rope(k_rope, cos, sin)
    scale = (QK_NOPE + QK_ROPE) ** -0.5
    outs = []
    for start in range(0, s, Q_BLOCK):
        end = start + Q_BLOCK
        sc = (jnp.einsum('bqhd,bkhd->bhqk', q_nope[:, start:end], k_nope[:, :end])
              + jnp.einsum('bqhd,bkd->bhqk', q_rope[:, start:end], k_rope[:, :end]))
        sc = sc.astype(jnp.float32) * scale
        qpos = jnp.arange(start, end)[:, None]
        kpos = jnp.arange(end)[None, :]
        sc = jnp.where(kpos <= qpos, sc, -jnp.inf)
        p = jax.nn.softmax(sc, axis=-1).astype(v.dtype)
        outs.append(jnp.einsum('bhqk,bkhd->bqhd', p, v[:, :end]))
    o = jnp.concatenate(outs, axis=1).reshape(b, s, MLA_HEADS * V_DIM)
    return o @ w_o


def causal_depthwise_conv(u, w, bias):
    k = w.shape[0]
    out = lax.conv_general_dilated(u, w[:, None, :].astype(u.dtype), window_strides=(1,),
                                   padding=[(k - 1, 0)],
                                   dimension_numbers=('NWC', 'WIO', 'NWC'),
                                   feature_group_count=u.shape[-1])
    return out + bias


def ssd_chunked_scan(x, dt, A, Bm, Cm):
    b, s, g, j, p = x.shape
    n = Bm.shape[-1]
    nc = s // SSM_CHUNK

    def to_chunks(t):
        return jnp.moveaxis(t.reshape(b, nc, SSM_CHUNK, *t.shape[2:]), 1, 0)

    causal = jnp.tril(jnp.ones((SSM_CHUNK, SSM_CHUNK), dtype=bool))[None, :, :, None, None]

    def step(state, inp):
        xc, dtc, bc, cc = inp
        cum = jnp.cumsum(dtc * A, axis=1)
        seg = cum[:, :, None] - cum[:, None, :]
        decay = jnp.exp(jnp.where(causal, seg, -jnp.inf))
        cb = jnp.einsum('btgn,bsgn->btsg', cc, bc)
        w = cb[..., None] * decay * dtc[:, None]
        y_intra = jnp.einsum('btsgj,bsgjp->btgjp', w, xc)
        y_inter = jnp.einsum('btgn,bgjpn->btgjp', cc, state) * jnp.exp(cum)[..., None]
        last = cum[:, -1]
        w_state = jnp.exp(last[:, None] - cum) * dtc
        new_state = (state * jnp.exp(last)[..., None, None]
                     + jnp.einsum('bsgjp,bsgn->bgjpn', w_state[..., None] * xc, bc))
        return new_state, y_intra + y_inter

    state0 = jnp.zeros((b, g, j, p, n), jnp.float32)
    _, ys = lax.scan(step, state0, (to_chunks(x), to_chunks(dt), to_chunks(Bm), to_chunks(Cm)))
    return jnp.moveaxis(ys, 0, 1).reshape(b, s, g, j, p)


def mamba2_mixer(h, w_in, conv_w, conv_b, dt_bias, A_log, D_skip, norm_w, w_out):
    b, s, _ = h.shape
    f32 = jnp.float32
    zxbcdt = h @ w_in
    z, xbc, dt = jnp.split(zxbcdt, [SSM_INNER, SSM_INNER + SSM_CONV_DIM], axis=-1)
    xbc = jax.nn.silu(causal_depthwise_conv(xbc, conv_w, conv_b))
    xs, Bm, Cm = jnp.split(xbc, [SSM_INNER, SSM_INNER + SSM_GROUPS * SSM_STATE], axis=-1)
    xs = xs.astype(f32).reshape(b, s, SSM_GROUPS, SSM_HPG, SSM_HEADDIM)
    Bm = Bm.astype(f32).reshape(b, s, SSM_GROUPS, SSM_STATE)
    Cm = Cm.astype(f32).reshape(b, s, SSM_GROUPS, SSM_STATE)
    dt = jax.nn.softplus(dt.astype(f32) + dt_bias.astype(f32)).reshape(b, s, SSM_GROUPS, SSM_HPG)
    A = -jnp.exp(A_log.astype(f32)).reshape(SSM_GROUPS, SSM_HPG)
    y = ssd_chunked_scan(xs, dt, A, Bm, Cm)
    y = y + D_skip.astype(f32).reshape(SSM_GROUPS, SSM_HPG)[..., None] * xs
    y = y.reshape(b, s, SSM_INNER)
    gated = (y * jax.nn.silu(z.astype(f32))).reshape(b, s, SSM_GROUPS, SSM_INNER // SSM_GROUPS)
    gated = gated * lax.rsqrt(jnp.mean(gated * gated, axis=-1, keepdims=True) + SSM_NORM_EPS)
    y = (gated.reshape(b, s, SSM_INNER) * norm_w.astype(f32)).astype(h.dtype)
    return y @ w_out


def swiglu(u, w_gate, w_up, w_down):
    return (jax.nn.silu(u @ w_gate) * (u @ w_up)) @ w_down


def setup_inputs(seed: int = 0) -> dict:
    key = jax.random.key(seed)
    ks = iter(jax.random.split(key, 32))
    f32 = jnp.float32
    out_scale = (2.0 * DEPTH) ** -0.5

    def w(shape, fan_in, scale=1.0):
        return jax.random.normal(next(ks), shape, f32) * (fan_in ** -0.5) * scale

    def gain(shape):
        return 1.0 + 0.02 * jax.random.normal(next(ks), shape, f32)

    x = jax.random.normal(next(ks), (BATCH, SEQ, D_MODEL), f32)
    offsets = jax.random.randint(next(ks), (BATCH, 1), 0, 4096, dtype=jnp.int32)
    positions = offsets + jnp.arange(SEQ, dtype=jnp.int32)[None, :]

    mix_norm = gain((DEPTH, D_MODEL))
    ffn_norm = gain((DEPTH, D_MODEL))
    final_norm = gain((D_MODEL,))

    mla_w_in = w((N_MLA, D_MODEL, MLA_IN_DIM), D_MODEL)
    mla_q_norm = gain((N_MLA, Q_LORA))
    mla_kv_norm = gain((N_MLA, KV_LORA))
    mla_w_uq = w((N_MLA, Q_LORA, MLA_HEADS * (QK_NOPE + QK_ROPE)), Q_LORA)
    mla_w_ukv = w((N_MLA, KV_LORA, MLA_HEADS * (QK_NOPE + V_DIM)), KV_LORA)
    mla_w_o = w((N_MLA, MLA_HEADS * V_DIM, D_MODEL), MLA_HEADS * V_DIM, out_scale)

    ssm_w_in = w((N_SSM, D_MODEL, SSM_IN_DIM), D_MODEL)
    ssm_conv_w = w((N_SSM, SSM_CONV, SSM_CONV_DIM), SSM_CONV)
    ssm_conv_b = 0.02 * jax.random.normal(next(ks), (N_SSM, SSM_CONV_DIM), f32)
    u = jax.random.uniform(next(ks), (N_SSM, SSM_HEADS), f32)
    dt0 = jnp.exp(u * (math.log(0.1) - math.log(0.001)) + math.log(0.001))
    ssm_dt_bias = dt0 + jnp.log(-jnp.expm1(-dt0))
    ssm_A_log = jnp.log(jax.random.uniform(next(ks), (N_SSM, SSM_HEADS), f32, 1.0, 16.0))
    ssm_D = gain((N_SSM, SSM_HEADS))
    ssm_norm = gain((N_SSM, SSM_INNER))
    ssm_w_out = w((N_SSM, SSM_INNER, D_MODEL), SSM_INNER, out_scale)

    ffn_w_gate = w((DEPTH, D_MODEL, FFN_HIDDEN), D_MODEL)
    ffn_w_up = w((DEPTH, D_MODEL, FFN_HIDDEN), D_MODEL)
    ffn_w_down = w((DEPTH, FFN_HIDDEN, D_MODEL), FFN_HIDDEN, out_scale)

    return {"x": x, "positions": positions, "mix_norm": mix_norm, "ffn_norm": ffn_norm,
            "final_norm": final_norm, "mla_w_in": mla_w_in, "mla_q_norm": mla_q_norm,
            "mla_kv_norm": mla_kv_norm, "mla_w_uq": mla_w_uq, "mla_w_ukv": mla_w_ukv,
            "mla_w_o": mla_w_o, "ssm_w_in": ssm_w_in, "ssm_conv_w": ssm_conv_w,
            "ssm_conv_b": ssm_conv_b, "ssm_dt_bias": ssm_dt_bias, "ssm_A_log": ssm_A_log,
            "ssm_D": ssm_D, "ssm_norm": ssm_norm, "ssm_w_out": ssm_w_out,
            "ffn_w_gate": ffn_w_gate, "ffn_w_up": ffn_w_up, "ffn_w_down": ffn_w_down}


def reference(x, positions, mix_norm, ffn_norm, final_norm, mla_w_in, mla_q_norm, mla_kv_norm,
              mla_w_uq, mla_w_ukv, mla_w_o, ssm_w_in, ssm_conv_w, ssm_conv_b, ssm_dt_bias,
              ssm_A_log, ssm_D, ssm_norm, ssm_w_out, ffn_w_gate, ffn_w_up, ffn_w_down):
    cos, sin = rope_tables(positions)
    h = x
    for i in range(DEPTH):
        u = rms_norm(h, mix_norm[i])
        k = i // N_MIXERS
        if i % N_MIXERS == 0:
            h = h + mla_mixer(u, cos, sin, mla_w_in[k], mla_q_norm[k], mla_kv_norm[k],
                              mla_w_uq[k], mla_w_ukv[k], mla_w_o[k])
        else:
            h = h + mamba2_mixer(u, ssm_w_in[k], ssm_conv_w[k], ssm_conv_b[k], ssm_dt_bias[k],
                                 ssm_A_log[k], ssm_D[k], ssm_norm[k], ssm_w_out[k])
        u = rms_norm(h, ffn_norm[i])
        h = h + swiglu(u, ffn_w_gate[i], ffn_w_up[i], ffn_w_down[i])
    return rms_norm(h, final_norm)
```

```python
import functools

import jax
import jax.numpy as jnp
from jax import lax
from jax.experimental import pallas as pl
from jax.experimental.pallas import tpu as pltpu

F32 = jnp.float32
BF16 = jnp.bfloat16

D_MODEL = 1024
DEPTH = 4
N_MIXERS = 2

MLA_HEADS = 8
Q_LORA = 512
KV_LORA = 256
QK_NOPE = 128
QK_ROPE = 64
V_DIM = 128
ROPE_THETA = 10000.0
ROPE_HALF = QK_ROPE // 2
HEAD_PAIRS = MLA_HEADS // 2
Q_PAIR_W = 2 * QK_NOPE + 2 * QK_ROPE
K_PAIR_W = 2 * QK_NOPE + 4 * QK_ROPE
QK_CAT = QK_NOPE + 2 * QK_ROPE
ATTN_SCALE = (QK_NOPE + QK_ROPE) ** -0.5

SSM_INNER = 2 * D_MODEL
SSM_HEADDIM = 64
SSM_HEADS = SSM_INNER // SSM_HEADDIM
SSM_GROUPS = 8
SSM_HPG = SSM_HEADS // SSM_GROUPS
SSM_STATE = 128
SSM_CONV = 4
SSM_CHUNK = 128
SSM_BC_DIM = SSM_GROUPS * SSM_STATE
SSM_CONV_DIM = SSM_INNER + 2 * SSM_BC_DIM
SSM_GROUP_W = SSM_HPG * SSM_HEADDIM
SSM_NORM_EPS = 1e-5

FFN_HIDDEN = 2816
NORM_EPS = 1e-6

LANES = 128
SUBLANES = 8
VMEM_LIMIT_BYTES = 56 * 1024 * 1024

TOKEN_TILE = 512
ATTN_TILE = 256
FFN_CHUNKS = (512, 512, 512, 512, 512, 256)
CONV_HALO = SUBLANES


def _params(*sem):
    return pltpu.CompilerParams(dimension_semantics=sem, vmem_limit_bytes=VMEM_LIMIT_BYTES)


def _resident(shape):
    return pl.BlockSpec(shape, lambda *_: (0,) * len(shape), pipeline_mode=pl.Buffered(1))


def _rms(x, g, eps):
    return x * lax.rsqrt(jnp.mean(x * x, axis=-1, keepdims=True) + eps) * g


def _silu(x):
    return x * jax.nn.sigmoid(x)


def _dot(a, b):
    return jnp.dot(a, b, preferred_element_type=F32)


def _dot_nt(a, b):
    return lax.dot_general(a, b, (((1,), (1,)), ((), ())), preferred_element_type=F32)


def _rope_kernel(pos_ref, inv_ref, sgn_ref, cos_ref, sin_ref):
    ang = pos_ref[...].astype(F32) * inv_ref[...]
    cos_ref[...] = jnp.cos(ang)
    sin_ref[...] = jnp.sin(ang) * sgn_ref[...]


def _rope_tables(positions):
    n = positions.size
    tm = min(n, 2048)
    inv = jnp.power(ROPE_THETA, -jnp.arange(ROPE_HALF, dtype=F32) / ROPE_HALF)
    inv = jnp.tile(inv, LANES // ROPE_HALF)[None, :]
    sgn = jnp.tile(jnp.concatenate([-jnp.ones(ROPE_HALF, F32), jnp.ones(ROPE_HALF, F32)]),
                   LANES // QK_ROPE)[None, :]
    row = pl.BlockSpec((tm, LANES), lambda i: (i, 0))
    return pl.pallas_call(
        _rope_kernel,
        grid=(n // tm,),
        in_specs=[pl.BlockSpec((tm, 1), lambda i: (i, 0)), _resident((1, LANES)), _resident((1, LANES))],
        out_specs=[row, row],
        out_shape=[jax.ShapeDtypeStruct((n, LANES), F32)] * 2,
        compiler_params=_params("parallel"),
        name="rope_tables",
    )(positions.reshape(n, 1), inv, sgn)


def _mla_proj_kernel(x_ref, g_ref, win_ref, qn_ref, kvn_ref, wuq_ref, wukv_ref, cos_ref, sin_ref,
                     q_ref, k_ref, v_ref):
    u = _rms(x_ref[...], g_ref[...], NORM_EPS).astype(BF16)
    lat = _dot(u, win_ref[...])
    cos = cos_ref[...]
    sin = sin_ref[...]
    kv0 = Q_LORA
    kr0 = Q_LORA + KV_LORA
    qn = _rms(lat[:, :Q_LORA], qn_ref[...], NORM_EPS).astype(BF16)
    kvn = _rms(lat[:, kv0:kr0], kvn_ref[...], NORM_EPS).astype(BF16)
    q = _dot(qn, wuq_ref[...])
    kv = _dot(kvn, wukv_ref[...])

    kr = lat[:, kr0:kr0 + LANES] * cos + lat[:, kr0 + LANES:kr0 + 2 * LANES] * sin
    lane = lax.broadcasted_iota(jnp.int32, kr.shape, 1)
    kr_lo = jnp.where(lane < QK_ROPE, kr, 0.0).astype(BF16)
    kr_hi = jnp.where(lane >= QK_ROPE, kr, 0.0).astype(BF16)

    nope_w = MLA_HEADS * QK_NOPE
    rope_w = MLA_HEADS * QK_ROPE
    for p in range(HEAD_PAIRS):
        a = 2 * p * QK_NOPE
        b = a + QK_NOPE
        r = nope_w + p * LANES
        qr = q[:, r:r + LANES] * cos + q[:, r + rope_w:r + rope_w + LANES] * sin
        q0 = p * Q_PAIR_W
        q_ref[:, q0:q0 + LANES] = (q[:, a:a + QK_NOPE] * ATTN_SCALE).astype(BF16)
        q_ref[:, q0 + LANES:q0 + 2 * LANES] = (qr * ATTN_SCALE).astype(BF16)
        q_ref[:, q0 + 2 * LANES:q0 + 3 * LANES] = (q[:, b:b + QK_NOPE] * ATTN_SCALE).astype(BF16)
        k0 = p * K_PAIR_W
        k_ref[:, k0:k0 + LANES] = kv[:, a:a + QK_NOPE].astype(BF16)
        k_ref[:, k0 + LANES:k0 + 2 * LANES] = kr_lo
        k_ref[:, k0 + 2 * LANES:k0 + 3 * LANES] = kr_hi
        k_ref[:, k0 + 3 * LANES:k0 + 4 * LANES] = kv[:, b:b + QK_NOPE].astype(BF16)
    v_ref[...] = kv[:, nope_w:].astype(BF16)


def _mla_proj(h, g, w_in, q_norm, kv_norm, w_uq, w_ukv, cos, sin):
    n = h.shape[0]
    tm = min(n, TOKEN_TILE)
    qw, kw, vw = HEAD_PAIRS * Q_PAIR_W, HEAD_PAIRS * K_PAIR_W, MLA_HEADS * V_DIM
    row = lambda w: pl.BlockSpec((tm, w), lambda i: (i, 0))
    return pl.pallas_call(
        _mla_proj_kernel,
        grid=(n // tm,),
        in_specs=[row(D_MODEL), _resident(g.shape), _resident(w_in.shape), _resident(q_norm.shape),
                  _resident(kv_norm.shape), _resident(w_uq.shape), _resident(w_ukv.shape),
                  row(LANES), row(LANES)],
        out_specs=[row(qw), row(kw), row(vw)],
        out_shape=[jax.ShapeDtypeStruct((n, qw), BF16), jax.ShapeDtypeStruct((n, kw), BF16),
                   jax.ShapeDtypeStruct((n, vw), BF16)],
        compiler_params=_params("parallel"),
        name="mla_proj",
    )(h, g, w_in, q_norm, kv_norm, w_uq, w_ukv, cos, sin)


def _attn_kernel(q_ref, k_ref, v_ref, o_ref):
    t = ATTN_TILE
    qi = pl.program_id(2)
    rows = lax.broadcasted_iota(jnp.int32, (t, t), 0)
    cols = lax.broadcasted_iota(jnp.int32, (t, t), 1)
    causal = cols <= rows

    for hh in range(2):
        qh = q_ref[:, hh * LANES:hh * LANES + QK_CAT]
        kc0 = hh * QK_CAT
        vc0 = hh * V_DIM

        def block(j, carry, masked):
            m, l, acc = carry
            start = pl.multiple_of(j * t, t)
            s = _dot_nt(qh, k_ref[pl.ds(start, t), kc0:kc0 + QK_CAT])
            if masked:
                s = jnp.where(causal, s, -jnp.inf)
            m_new = jnp.maximum(m, jnp.max(s, axis=-1, keepdims=True))
            alpha = jnp.exp(m - m_new)
            p = jnp.exp(s - m_new)
            l = alpha * l + jnp.sum(p, axis=-1, keepdims=True)
            acc = alpha * acc + _dot(p.astype(BF16), v_ref[pl.ds(start, t), vc0:vc0 + V_DIM])
            return m_new, l, acc

        init = (jnp.full((t, 1), -jnp.inf, F32), jnp.zeros((t, 1), F32), jnp.zeros((t, V_DIM), F32))
        carry = lax.fori_loop(0, qi, functools.partial(block, masked=False), init)
        _, l, acc = block(qi, carry, True)
        o_ref[:, vc0:vc0 + V_DIM] = (acc / l).astype(BF16)


def _attention(q, k, v):
    b, s, _ = q.shape
    t = ATTN_TILE
    return pl.pallas_call(
        _attn_kernel,
        grid=(b, HEAD_PAIRS, s // t),
        in_specs=[pl.BlockSpec((None, t, Q_PAIR_W), lambda bi, p, i: (bi, i, p)),
                  pl.BlockSpec((None, s, K_PAIR_W), lambda bi, p, i: (bi, 0, p)),
                  pl.BlockSpec((None, s, 2 * V_DIM), lambda bi, p, i: (bi, 0, p))],
        out_specs=pl.BlockSpec((None, t, 2 * V_DIM), lambda bi, p, i: (bi, i, p)),
        out_shape=jax.ShapeDtypeStruct((b, s, MLA_HEADS * V_DIM), BF16),
        compiler_params=_params("parallel", "parallel", "arbitrary"),
        name="mla_attention",
    )(q, k, v)


def _proj_residual_kernel(h_ref, a_ref, w_ref, o_ref):
    o_ref[...] = h_ref[...] + _dot(a_ref[...], w_ref[...])


def _proj_residual(h, a, w):
    n, kdim = a.shape
    tm = min(n, TOKEN_TILE)
    return pl.pallas_call(
        _proj_residual_kernel,
        grid=(n // tm,),
        in_specs=[pl.BlockSpec((tm, D_MODEL), lambda i: (i, 0)), pl.BlockSpec((tm, kdim), lambda i: (i, 0)),
                  _resident(w.shape)],
        out_specs=pl.BlockSpec((tm, D_MODEL), lambda i: (i, 0)),
        out_shape=jax.ShapeDtypeStruct((n, D_MODEL), F32),
        compiler_params=_params("parallel"),
        name="proj_residual",
    )(h, a, w)


def _ffn_kernel(x_ref, g_ref, wg_ref, wu_ref, wd_ref, fg_ref, o_ref, act_ref, *, final_norm):
    x = x_ref[...]
    u = _rms(x, g_ref[...], NORM_EPS).astype(BF16)
    off = 0
    for c in FFN_CHUNKS:
        gate = _dot(u, wg_ref[:, off:off + c])
        up = _dot(u, wu_ref[:, off:off + c])
        act_ref[:, off:off + c] = (_silu(gate) * up).astype(BF16)
        off += c
    y = x + _dot(act_ref[...], wd_ref[...])
    if final_norm:
        y = _rms(y, fg_ref[...], NORM_EPS)
    o_ref[...] = y


def _ffn(h, g, w_gate, w_up, w_down, final_g, final_norm):
    n = h.shape[0]
    tm = min(n, TOKEN_TILE)
    row = pl.BlockSpec((tm, D_MODEL), lambda i: (i, 0))
    return pl.pallas_call(
        functools.partial(_ffn_kernel, final_norm=final_norm),
        grid=(n // tm,),
        in_specs=[row, _resident(g.shape), _resident(w_gate.shape), _resident(w_up.shape),
                  _resident(w_down.shape), _resident(final_g.shape)],
        out_specs=row,
        out_shape=jax.ShapeDtypeStruct((n, D_MODEL), F32),
        scratch_shapes=[pltpu.VMEM((tm, FFN_HIDDEN), BF16)],
        compiler_params=_params("parallel"),
        name="ffn_final" if final_norm else "ffn",
    )(h, g, w_gate, w_up, w_down, final_g)


def _ssm_in_kernel(x_ref, g_ref, wz_ref, wx_ref, wdt_ref, z_ref, xbc_ref, dt_ref):
    u = _rms(x_ref[...], g_ref[...], NORM_EPS).astype(BF16)
    z_ref[...] = _dot(u, wz_ref[...]).astype(BF16)
    step = SSM_CONV_DIM // 4
    for c in range(0, SSM_CONV_DIM, step):
        xbc_ref[:, c:c + step] = _dot(u, wx_ref[:, c:c + step]).astype(BF16)
    dt_ref[...] = _dot(u, wdt_ref[...])


def _ssm_in(h, g, wz, wx, wdt):
    n = h.shape[0]
    tm = min(n, TOKEN_TILE)
    row = lambda w: pl.BlockSpec((tm, w), lambda i: (i, 0))
    return pl.pallas_call(
        _ssm_in_kernel,
        grid=(n // tm,),
        in_specs=[row(D_MODEL), _resident(g.shape), _resident(wz.shape), _resident(wx.shape),
                  _resident(wdt.shape)],
        out_specs=[row(SSM_INNER), row(SSM_CONV_DIM), row(LANES)],
        out_shape=[jax.ShapeDtypeStruct((n, SSM_INNER), BF16), jax.ShapeDtypeStruct((n, SSM_CONV_DIM), BF16),
                   jax.ShapeDtypeStruct((n, LANES), F32)],
        compiler_params=_params("parallel"),
        name="ssm_in",
    )(h, g, wz, wx, wdt)


def _split3(x):
    hi = x.astype(BF16)
    r = x - hi.astype(F32)
    mid = r.astype(BF16)
    lo = (r - mid.astype(F32)).astype(BF16)
    return hi, mid, lo


def _ssd_kernel(xbc_ref, z_ref, dt_ref, cw_ref, cb_ref, dtb_ref, alog_ref, dskip_ref, nw_ref, tril_ref,
                y_ref, state_ref, ext_ref):
    L = SSM_CHUNK
    H0 = CONV_HALO

    @pl.when(pl.program_id(1) == 0)
    def _():
        state_ref[...] = jnp.zeros_like(state_ref)
        ext_ref[0:H0, :] = jnp.zeros((H0, SSM_CONV_DIM), F32)

    ext_ref[H0:H0 + L, :] = xbc_ref[...].astype(F32)

    def conv_silu(c0, w):
        acc = cb_ref[:, c0:c0 + w]
        for k in range(SSM_CONV):
            r0 = H0 - (SSM_CONV - 1) + k
            acc = acc + cw_ref[k:k + 1, c0:c0 + w] * ext_ref[r0:r0 + L, c0:c0 + w]
        return _silu(acc)

    dtv = dt_ref[...] + dtb_ref[...]
    dt = jnp.maximum(dtv, 0.0) + jnp.log1p(jnp.exp(-jnp.abs(dtv)))
    da = dt * (-jnp.exp(alog_ref[...]))
    tril = tril_ref[...]
    hi, mid, lo = _split3(da)
    cum = _dot(tril, hi) + _dot(tril, mid) + _dot(tril, lo)
    cum_t = cum.T
    dt_t = dt.T
    last_t = cum_t[:, L - 1:L]
    ws_t = jnp.exp(last_t - cum_t) * dt_t
    elast_t = jnp.exp(last_t)

    rows = lax.broadcasted_iota(jnp.int32, (L, L), 0)
    cols = lax.broadcasted_iota(jnp.int32, (L, L), 1)
    causal = rows >= cols
    lane = lax.broadcasted_iota(jnp.int32, (L, LANES), 1)
    first = lane < SSM_HEADDIM

    for g in range(SSM_GROUPS):
        x0 = g * SSM_GROUP_W
        bmat = conv_silu(SSM_INNER + g * SSM_STATE, SSM_STATE)
        cmat = conv_silu(SSM_INNER + SSM_BC_DIM + g * SSM_STATE, SSM_STATE)
        cb16 = cmat.astype(BF16)
        cb = _dot_nt(cb16, bmat.astype(BF16))
        b_t = bmat.T
        y_in = _dot(cb16, state_ref[g].astype(BF16))
        gated = []
        for pr in range(SSM_HPG // 2):
            c0 = x0 + pr * LANES
            xp = conv_silu(c0, LANES)
            xp16 = xp.astype(BF16)
            y_intra, s_new, e_col, e_last = [], [], [], []
            for j in range(2):
                hd = g * SSM_HPG + 2 * pr + j
                col = cum[:, hd:hd + 1]
                row = cum_t[hd:hd + 1, :]
                decay = jnp.exp(jnp.where(causal, col - row, -jnp.inf))
                w = (cb * decay * dt_t[hd:hd + 1, :]).astype(BF16)
                y_intra.append(_dot(w, xp16))
                bs = (b_t * ws_t[hd:hd + 1, :]).astype(BF16)
                s_new.append(_dot(bs, xp16))
                e_col.append(jnp.exp(col))
                e_last.append(elast_t[hd:hd + 1, :])
            sl = slice(pr * LANES, (pr + 1) * LANES)
            y = (jnp.where(first, y_intra[0], y_intra[1])
                 + y_in[:, sl] * jnp.where(first, e_col[0], e_col[1])
                 + dskip_ref[:, c0:c0 + LANES] * xp)
            state_ref[g, :, sl] = (state_ref[g, :, sl] * jnp.where(first, e_last[0], e_last[1])
                                   + jnp.where(first, s_new[0], s_new[1]))
            zp = z_ref[:, c0:c0 + LANES].astype(F32)
            gated.append(y * _silu(zp))
        gated = jnp.concatenate(gated, axis=1)
        ms = jnp.mean(gated * gated, axis=-1, keepdims=True)
        y_ref[:, x0:x0 + SSM_GROUP_W] = (gated * lax.rsqrt(ms + SSM_NORM_EPS)
                                         * nw_ref[:, x0:x0 + SSM_GROUP_W]).astype(y_ref.dtype)

    ext_ref[0:H0, :] = ext_ref[L:L + H0, :]


def _ssd(xbc, z, dt, conv_w, conv_b, dt_bias, a_log, d_skip, norm_w, tril):
    b, s, _ = xbc.shape
    L = SSM_CHUNK
    blk = lambda w: pl.BlockSpec((None, L, w), lambda bi, c: (bi, c, 0))
    return pl.pallas_call(
        _ssd_kernel,
        grid=(b, s // L),
        in_specs=[blk(SSM_CONV_DIM), blk(SSM_INNER), blk(LANES), _resident(conv_w.shape),
                  _resident(conv_b.shape), _resident(dt_bias.shape), _resident(a_log.shape),
                  _resident(d_skip.shape), _resident(norm_w.shape), _resident(tril.shape)],
        out_specs=blk(SSM_INNER),
        out_shape=jax.ShapeDtypeStruct((b, s, SSM_INNER), BF16),
        scratch_shapes=[pltpu.VMEM((SSM_GROUPS, SSM_STATE, SSM_GROUP_W), F32),
                        pltpu.VMEM((CONV_HALO + L, SSM_CONV_DIM), F32)],
        compiler_params=_params("parallel", "arbitrary"),
        name="ssd",
    )(xbc, z, dt, conv_w, conv_b, dt_bias, a_log, d_skip, norm_w, tril)


def _swap_halves(w):
    lead = w.shape[:-1]
    w = w.reshape(*lead, -1, 2, ROPE_HALF)
    return jnp.flip(w, axis=-2).reshape(*lead, -1)


def _mla_weights(w_in, w_uq, w_ukv):
    kr = w_in[:, Q_LORA + KV_LORA:]
    kr_sw = _swap_halves(kr)
    w_in2 = jnp.concatenate([w_in[:, :Q_LORA + KV_LORA], kr, kr, kr_sw, kr_sw], axis=1)
    uq = w_uq.reshape(Q_LORA, MLA_HEADS, QK_NOPE + QK_ROPE)
    q_nope = uq[:, :, :QK_NOPE].reshape(Q_LORA, -1)
    q_rope = uq[:, :, QK_NOPE:].reshape(Q_LORA, -1)
    w_uq2 = jnp.concatenate([q_nope, q_rope, _swap_halves(q_rope)], axis=1)
    ukv = w_ukv.reshape(KV_LORA, MLA_HEADS, QK_NOPE + V_DIM)
    w_ukv2 = jnp.concatenate([ukv[:, :, :QK_NOPE].reshape(KV_LORA, -1),
                              ukv[:, :, QK_NOPE:].reshape(KV_LORA, -1)], axis=1)
    return w_in2.astype(BF16), w_uq2.astype(BF16), w_ukv2.astype(BF16)


def _pad_lanes(v):
    return jnp.pad(v, (0, LANES - v.shape[0]))[None, :]


def kernel(x, positions, mix_norm, ffn_norm, final_norm, mla_w_in, mla_q_norm, mla_kv_norm, mla_w_uq,
           mla_w_ukv, mla_w_o, ssm_w_in, ssm_conv_w, ssm_conv_b, ssm_dt_bias, ssm_A_log, ssm_D, ssm_norm,
           ssm_w_out, ffn_w_gate, ffn_w_up, ffn_w_down):
    b, s, d = x.shape
    n = b * s
    cos, sin = _rope_tables(positions)
    tril = jnp.tril(jnp.ones((SSM_CHUNK, SSM_CHUNK), BF16))
    final_g = final_norm[None, :]

    h = x.reshape(n, d)
    for i in range(DEPTH):
        k = i // N_MIXERS
        g_mix = mix_norm[i][None, :]
        if i % N_MIXERS == 0:
            w_in, w_uq, w_ukv = _mla_weights(mla_w_in[k], mla_w_uq[k], mla_w_ukv[k])
            q, kk, v = _mla_proj(h, g_mix, w_in, mla_q_norm[k][None, :], mla_kv_norm[k][None, :],
                                 w_uq, w_ukv, cos, sin)
            o = _attention(q.reshape(b, s, -1), kk.reshape(b, s, -1), v.reshape(b, s, -1))
            h = _proj_residual(h, o.reshape(n, -1), mla_w_o[k].astype(BF16))
        else:
            w = ssm_w_in[k]
            wz = w[:, :SSM_INNER].astype(BF16)
            wx = w[:, SSM_INNER:SSM_INNER + SSM_CONV_DIM].astype(BF16)
            wdt = jnp.pad(w[:, SSM_INNER + SSM_CONV_DIM:], ((0, 0), (0, LANES - SSM_HEADS))).astype(BF16)
            z, xbc, dt = _ssm_in(h, g_mix, wz, wx, wdt)
            y = _ssd(xbc.reshape(b, s, -1), z.reshape(b, s, -1), dt.reshape(b, s, -1),
                     ssm_conv_w[k], ssm_conv_b[k][None, :], _pad_lanes(ssm_dt_bias[k]),
                     _pad_lanes(ssm_A_log[k]), jnp.repeat(ssm_D[k], SSM_HEADDIM)[None, :],
                     ssm_norm[k][None, :], tril)
            h = _proj_residual(h, y.reshape(n, -1), ssm_w_out[k].astype(BF16))
        h = _ffn(h, ffn_norm[i][None, :], ffn_w_gate[i].astype(BF16), ffn_w_up[i].astype(BF16),
                 ffn_w_down[i].astype(BF16), final_g, final_norm=(i == DEPTH - 1))
    return h.reshape(b, s, d)
```

```python
import functools

import jax
import jax.numpy as jnp
from jax import lax
from jax.experimental import pallas as pl
from jax.experimental.pallas import tpu as pltpu

F32 = jnp.float32
BF16 = jnp.bfloat16

D_MODEL = 1024
DEPTH = 4
N_MIXERS = 2

MLA_HEADS = 8
Q_LORA = 512
KV_LORA = 256
QK_NOPE = 128
QK_ROPE = 64
V_DIM = 128
ROPE_THETA = 10000.0
ROPE_HALF = QK_ROPE // 2
HEAD_PAIRS = MLA_HEADS // 2
Q_PAIR_W = 2 * QK_NOPE + 2 * QK_ROPE
K_PAIR_W = 2 * QK_NOPE + 4 * QK_ROPE
QK_CAT = QK_NOPE + 2 * QK_ROPE
ATTN_SCALE = (QK_NOPE + QK_ROPE) ** -0.5

SSM_INNER = 2 * D_MODEL
SSM_HEADDIM = 64
SSM_HEADS = SSM_INNER // SSM_HEADDIM
SSM_GROUPS = 8
SSM_HPG = SSM_HEADS // SSM_GROUPS
SSM_STATE = 128
SSM_CONV = 4
SSM_CHUNK = 128
SSM_BC_DIM = SSM_GROUPS * SSM_STATE
SSM_CONV_DIM = SSM_INNER + 2 * SSM_BC_DIM
SSM_GROUP_W = SSM_HPG * SSM_HEADDIM
SSM_NORM_EPS = 1e-5

FFN_HIDDEN = 2816
NORM_EPS = 1e-6

LANES = 128
SUBLANES = 8
VMEM_LIMIT_BYTES = 56 * 1024 * 1024

TOKEN_TILE = 512
ATTN_TILE = 256
FFN_CHUNKS = (512, 512, 512, 512, 512, 256)
CONV_HALO = SUBLANES


def _params(*sem):
    return pltpu.CompilerParams(dimension_semantics=sem, vmem_limit_bytes=VMEM_LIMIT_BYTES)


def _resident(shape):
    return pl.BlockSpec(shape, lambda *_: (0,) * len(shape), pipeline_mode=pl.Buffered(1))


def _rms(x, g, eps):
    return x * lax.rsqrt(jnp.mean(x * x, axis=-1, keepdims=True) + eps) * g


def _silu(x):
    return x * jax.nn.sigmoid(x)


def _dot(a, b):
    return jnp.dot(a, b, preferred_element_type=F32)


def _dot_nt(a, b):
    return lax.dot_general(a, b, (((1,), (1,)), ((), ())), preferred_element_type=F32)


def _rope_kernel(pos_ref, inv_ref, sgn_ref, cos_ref, sin_ref):
    ang = pos_ref[...].astype(F32) * inv_ref[...]
    cos_ref[...] = jnp.cos(ang)
    sin_ref[...] = jnp.sin(ang) * sgn_ref[...]


def _rope_tables(positions):
    n = positions.size
    tm = min(n, 2048)
    inv = jnp.power(ROPE_THETA, -jnp.arange(ROPE_HALF, dtype=F32) / ROPE_HALF)
    inv = jnp.tile(inv, LANES // ROPE_HALF)[None, :]
    sgn = jnp.tile(jnp.concatenate([-jnp.ones(ROPE_HALF, F32), jnp.ones(ROPE_HALF, F32)]),
                   LANES // QK_ROPE)[None, :]
    row = pl.BlockSpec((tm, LANES), lambda i: (i, 0))
    return pl.pallas_call(
        _rope_kernel,
        grid=(n // tm,),
        in_specs=[pl.BlockSpec((tm, 1), lambda i: (i, 0)), _resident((1, LANES)), _resident((1, LANES))],
        out_specs=[row, row],
        out_shape=[jax.ShapeDtypeStruct((n, LANES), F32)] * 2,
        compiler_params=_params("parallel"),
        name="rope_tables",
    )(positions.reshape(n, 1), inv, sgn)


def _mla_proj_kernel(x_ref, g_ref, win_ref, qn_ref, kvn_ref, wuq_ref, wukv_ref, cos_ref, sin_ref,
                     q_ref, k_ref, v_ref):
    u = _rms(x_ref[...], g_ref[...], NORM_EPS).astype(BF16)
    lat = _dot(u, win_ref[...])
    cos = cos_ref[...]
    sin = sin_ref[...]
    kv0 = Q_LORA
    kr0 = Q_LORA + KV_LORA
    qn = _rms(lat[:, :Q_LORA], qn_ref[...], NORM_EPS).astype(BF16)
    kvn = _rms(lat[:, kv0:kr0], kvn_ref[...], NORM_EPS).astype(BF16)
    q = _dot(qn, wuq_ref[...])
    kv = _dot(kvn, wukv_ref[...])

    kr = lat[:, kr0:kr0 + LANES] * cos + lat[:, kr0 + LANES:kr0 + 2 * LANES] * sin
    lane = lax.broadcasted_iota(jnp.int32, kr.shape, 1)
    kr_lo = jnp.where(lane < QK_ROPE, kr, 0.0).astype(BF16)
    kr_hi = jnp.where(lane >= QK_ROPE, kr, 0.0).astype(BF16)

    nope_w = MLA_HEADS * QK_NOPE
    rope_w = MLA_HEADS * QK_ROPE
    for p in range(HEAD_PAIRS):
        a = 2 * p * QK_NOPE
        b = a + QK_NOPE
        r = nope_w + p * LANES
        qr = q[:, r:r + LANES] * cos + q[:, r + rope_w:r + rope_w + LANES] * sin
        q0 = p * Q_PAIR_W
        q_ref[:, q0:q0 + LANES] = (q[:, a:a + QK_NOPE] * ATTN_SCALE).astype(BF16)
        q_ref[:, q0 + LANES:q0 + 2 * LANES] = (qr * ATTN_SCALE).astype(BF16)
        q_ref[:, q0 + 2 * LANES:q0 + 3 * LANES] = (q[:, b:b + QK_NOPE] * ATTN_SCALE).astype(BF16)
        k0 = p * K_PAIR_W
        k_ref[:, k0:k0 + LANES] = kv[:, a:a + QK_NOPE].astype(BF16)
        k_ref[:, k0 + LANES:k0 + 2 * LANES] = kr_lo
        k_ref[:, k0 + 2 * LANES:k0 + 3 * LANES] = kr_hi
        k_ref[:, k0 + 3 * LANES:k0 + 4 * LANES] = kv[:, b:b + QK_NOPE].astype(BF16)
    v_ref[...] = kv[:, nope_w:].astype(BF16)


def _mla_proj(h, g, w_in, q_norm, kv_norm, w_uq, w_ukv, cos, sin):
    n = h.shape[0]
    tm = min(n, TOKEN_TILE)
    qw, kw, vw = HEAD_PAIRS * Q_PAIR_W, HEAD_PAIRS * K_PAIR_W, MLA_HEADS * V_DIM
    row = lambda w: pl.BlockSpec((tm, w), lambda i: (i, 0))
    return pl.pallas_call(
        _mla_proj_kernel,
        grid=(n // tm,),
        in_specs=[row(D_MODEL), _resident(g.shape), _resident(w_in.shape), _resident(q_norm.shape),
                  _resident(kv_norm.shape), _resident(w_uq.shape), _resident(w_ukv.shape),
                  row(LANES), row(LANES)],
        out_specs=[row(qw), row(kw), row(vw)],
        out_shape=[jax.ShapeDtypeStruct((n, qw), BF16), jax.ShapeDtypeStruct((n, kw), BF16),
                   jax.ShapeDtypeStruct((n, vw), BF16)],
        compiler_params=_params("parallel"),
        name="mla_proj",
    )(h, g, w_in, q_norm, kv_norm, w_uq, w_ukv, cos, sin)


def _attn_kernel(q_ref, k_ref, v_ref, o_ref):
    t = ATTN_TILE
    rows = lax.broadcasted_iota(jnp.int32, (t, t), 0)
    cols = lax.broadcasted_iota(jnp.int32, (t, t), 1)
    causal = cols <= rows

    for i in range(q_ref.shape[0] // t):
        r0 = i * t
        for hh in range(2):
            qh = q_ref[r0:r0 + t, hh * LANES:hh * LANES + QK_CAT]
            kc = slice(hh * QK_CAT, (hh + 1) * QK_CAT)
            vc = slice(hh * V_DIM, (hh + 1) * V_DIM)
            s_d = jnp.where(causal, _dot_nt(qh, k_ref[r0:r0 + t, kc]), -jnp.inf)
            m = jnp.max(s_d, axis=-1, keepdims=True)
            if i:
                s_p = _dot_nt(qh, k_ref[0:r0, kc])
                m = jnp.maximum(m, jnp.max(s_p, axis=-1, keepdims=True))
            p_d = jnp.exp(s_d - m)
            l = jnp.sum(p_d, axis=-1, keepdims=True)
            o = _dot(p_d.astype(BF16), v_ref[r0:r0 + t, vc])
            if i:
                p_p = jnp.exp(s_p - m)
                l = l + jnp.sum(p_p, axis=-1, keepdims=True)
                o = o + _dot(p_p.astype(BF16), v_ref[0:r0, vc])
            o_ref[r0:r0 + t, vc] = (o / l).astype(BF16)


def _attention(q, k, v):
    b, s, _ = q.shape
    blk = lambda w: pl.BlockSpec((None, s, w), lambda bi, p: (bi, 0, p))
    return pl.pallas_call(
        _attn_kernel,
        grid=(b, HEAD_PAIRS),
        in_specs=[blk(Q_PAIR_W), blk(K_PAIR_W), blk(2 * V_DIM)],
        out_specs=blk(2 * V_DIM),
        out_shape=jax.ShapeDtypeStruct((b, s, MLA_HEADS * V_DIM), BF16),
        compiler_params=_params("parallel", "parallel"),
        name="mla_attention",
    )(q, k, v)


def _proj_residual_kernel(h_ref, a_ref, w_ref, o_ref):
    o_ref[...] = h_ref[...] + _dot(a_ref[...], w_ref[...])


def _proj_residual(h, a, w):
    n, kdim = a.shape
    tm = min(n, TOKEN_TILE)
    return pl.pallas_call(
        _proj_residual_kernel,
        grid=(n // tm,),
        in_specs=[pl.BlockSpec((tm, D_MODEL), lambda i: (i, 0)), pl.BlockSpec((tm, kdim), lambda i: (i, 0)),
                  _resident(w.shape)],
        out_specs=pl.BlockSpec((tm, D_MODEL), lambda i: (i, 0)),
        out_shape=jax.ShapeDtypeStruct((n, D_MODEL), F32),
        compiler_params=_params("parallel"),
        name="proj_residual",
    )(h, a, w)


def _ffn_kernel(x_ref, g_ref, wg_ref, wu_ref, wd_ref, fg_ref, o_ref, act_ref, *, final_norm):
    x = x_ref[...]
    u = _rms(x, g_ref[...], NORM_EPS).astype(BF16)
    off = 0
    for c in FFN_CHUNKS:
        gate = _dot(u, wg_ref[:, off:off + c])
        up = _dot(u, wu_ref[:, off:off + c])
        act_ref[:, off:off + c] = (_silu(gate) * up).astype(BF16)
        off += c
    y = x + _dot(act_ref[...], wd_ref[...])
    if final_norm:
        y = _rms(y, fg_ref[...], NORM_EPS)
    o_ref[...] = y


def _ffn(h, g, w_gate, w_up, w_down, final_g, final_norm):
    n = h.shape[0]
    tm = min(n, TOKEN_TILE)
    row = pl.BlockSpec((tm, D_MODEL), lambda i: (i, 0))
    return pl.pallas_call(
        functools.partial(_ffn_kernel, final_norm=final_norm),
        grid=(n // tm,),
        in_specs=[row, _resident(g.shape), _resident(w_gate.shape), _resident(w_up.shape),
                  _resident(w_down.shape), _resident(final_g.shape)],
        out_specs=row,
        out_shape=jax.ShapeDtypeStruct((n, D_MODEL), F32),
        scratch_shapes=[pltpu.VMEM((tm, FFN_HIDDEN), BF16)],
        compiler_params=_params("parallel"),
        name="ffn_final" if final_norm else "ffn",
    )(h, g, w_gate, w_up, w_down, final_g)


def _ssm_in_kernel(x_ref, g_ref, wz_ref, wx_ref, wdt_ref, z_ref, xbc_ref, dt_ref):
    u = _rms(x_ref[...], g_ref[...], NORM_EPS).astype(BF16)
    z_ref[...] = _dot(u, wz_ref[...]).astype(BF16)
    step = SSM_CONV_DIM // 4
    for c in range(0, SSM_CONV_DIM, step):
        xbc_ref[:, c:c + step] = _dot(u, wx_ref[:, c:c + step]).astype(BF16)
    dt_ref[...] = _dot(u, wdt_ref[...])


def _ssm_in(h, g, wz, wx, wdt):
    n = h.shape[0]
    tm = min(n, TOKEN_TILE)
    row = lambda w: pl.BlockSpec((tm, w), lambda i: (i, 0))
    return pl.pallas_call(
        _ssm_in_kernel,
        grid=(n // tm,),
        in_specs=[row(D_MODEL), _resident(g.shape), _resident(wz.shape), _resident(wx.shape),
                  _resident(wdt.shape)],
        out_specs=[row(SSM_INNER), row(SSM_CONV_DIM), row(LANES)],
        out_shape=[jax.ShapeDtypeStruct((n, SSM_INNER), BF16), jax.ShapeDtypeStruct((n, SSM_CONV_DIM), BF16),
                   jax.ShapeDtypeStruct((n, LANES), F32)],
        compiler_params=_params("parallel"),
        name="ssm_in",
    )(h, g, wz, wx, wdt)


def _split3(x):
    hi = x.astype(BF16)
    r = x - hi.astype(F32)
    mid = r.astype(BF16)
    lo = (r - mid.astype(F32)).astype(BF16)
    return hi, mid, lo


def _ssd_kernel(xbc_ref, z_ref, dt_ref, cw_ref, cb_ref, dtb_ref, alog_ref, dskip_ref, nw_ref, tril_ref,
                y_ref, state_ref, ext_ref):
    L = SSM_CHUNK
    H0 = CONV_HALO

    @pl.when(pl.program_id(1) == 0)
    def _():
        state_ref[...] = jnp.zeros_like(state_ref)
        ext_ref[0:H0, :] = jnp.zeros((H0, SSM_CONV_DIM), F32)

    ext_ref[H0:H0 + L, :] = xbc_ref[...].astype(F32)

    def conv_silu(c0, w):
        acc = cb_ref[:, c0:c0 + w]
        for k in range(SSM_CONV):
            r0 = H0 - (SSM_CONV - 1) + k
            acc = acc + cw_ref[k:k + 1, c0:c0 + w] * ext_ref[r0:r0 + L, c0:c0 + w]
        return _silu(acc)

    dtv = dt_ref[...] + dtb_ref[...]
    dt = jnp.maximum(dtv, 0.0) + jnp.log1p(jnp.exp(-jnp.abs(dtv)))
    da = dt * (-jnp.exp(alog_ref[...]))
    tril = tril_ref[...]
    hi, mid, lo = _split3(da)
    cum = _dot(tril, hi) + _dot(tril, mid) + _dot(tril, lo)
    cum_t = cum.T
    dt_t = dt.T
    last_t = cum_t[:, L - 1:L]
    ws_t = jnp.exp(last_t - cum_t) * dt_t
    elast_t = jnp.exp(last_t)

    rows = lax.broadcasted_iota(jnp.int32, (L, L), 0)
    cols = lax.broadcasted_iota(jnp.int32, (L, L), 1)
    causal = rows >= cols
    lane = lax.broadcasted_iota(jnp.int32, (L, LANES), 1)
    first = lane < SSM_HEADDIM

    for g in range(SSM_GROUPS):
        x0 = g * SSM_GROUP_W
        bmat = conv_silu(SSM_INNER + g * SSM_STATE, SSM_STATE)
        cmat = conv_silu(SSM_INNER + SSM_BC_DIM + g * SSM_STATE, SSM_STATE)
        cb16 = cmat.astype(BF16)
        cb = _dot_nt(cb16, bmat.astype(BF16))
        b_t = bmat.T
        y_in = _dot(cb16, state_ref[g].astype(BF16))
        gated = []
        for pr in range(SSM_HPG // 2):
            c0 = x0 + pr * LANES
            xp = conv_silu(c0, LANES)
            xp16 = xp.astype(BF16)
            y_intra, s_new, e_col, e_last = [], [], [], []
            for j in range(2):
                hd = g * SSM_HPG + 2 * pr + j
                col = cum[:, hd:hd + 1]
                row = cum_t[hd:hd + 1, :]
                decay = jnp.exp(jnp.where(causal, col - row, -jnp.inf))
                w = (cb * decay * dt_t[hd:hd + 1, :]).astype(BF16)
                y_intra.append(_dot(w, xp16))
                bs = (b_t * ws_t[hd:hd + 1, :]).astype(BF16)
                s_new.append(_dot(bs, xp16))
                e_col.append(jnp.exp(col))
                e_last.append(elast_t[hd:hd + 1, :])
            sl = slice(pr * LANES, (pr + 1) * LANES)
            y = (jnp.where(first, y_intra[0], y_intra[1])
                 + y_in[:, sl] * jnp.where(first, e_col[0], e_col[1])
                 + dskip_ref[:, c0:c0 + LANES] * xp)
            state_ref[g, :, sl] = (state_ref[g, :, sl] * jnp.where(first, e_last[0], e_last[1])
                                   + jnp.where(first, s_new[0], s_new[1]))
            zp = z_ref[:, c0:c0 + LANES].astype(F32)
            gated.append(y * _silu(zp))
        gated = jnp.concatenate(gated, axis=1)
        ms = jnp.mean(gated * gated, axis=-1, keepdims=True)
        y_ref[:, x0:x0 + SSM_GROUP_W] = (gated * lax.rsqrt(ms + SSM_NORM_EPS)
                                         * nw_ref[:, x0:x0 + SSM_GROUP_W]).astype(y_ref.dtype)

    ext_ref[0:H0, :] = ext_ref[L:L + H0, :]


def _ssd(xbc, z, dt, conv_w, conv_b, dt_bias, a_log, d_skip, norm_w, tril):
    b, s, _ = xbc.shape
    L = SSM_CHUNK
    blk = lambda w: pl.BlockSpec((None, L, w), lambda bi, c: (bi, c, 0))
    return pl.pallas_call(
        _ssd_kernel,
        grid=(b, s // L),
        in_specs=[blk(SSM_CONV_DIM), blk(SSM_INNER), blk(LANES), _resident(conv_w.shape),
                  _resident(conv_b.shape), _resident(dt_bias.shape), _resident(a_log.shape),
                  _resident(d_skip.shape), _resident(norm_w.shape), _resident(tril.shape)],
        out_specs=blk(SSM_INNER),
        out_shape=jax.ShapeDtypeStruct((b, s, SSM_INNER), BF16),
        scratch_shapes=[pltpu.VMEM((SSM_GROUPS, SSM_STATE, SSM_GROUP_W), F32),
                        pltpu.VMEM((CONV_HALO + L, SSM_CONV_DIM), F32)],
        compiler_params=_params("parallel", "arbitrary"),
        name="ssd",
    )(xbc, z, dt, conv_w, conv_b, dt_bias, a_log, d_skip, norm_w, tril)


def _swap_halves(w):
    lead = w.shape[:-1]
    w = w.reshape(*lead, -1, 2, ROPE_HALF)
    return jnp.flip(w, axis=-2).reshape(*lead, -1)


def _mla_weights(w_in, w_uq, w_ukv):
    kr = w_in[:, Q_LORA + KV_LORA:]
    kr_sw = _swap_halves(kr)
    w_in2 = jnp.concatenate([w_in[:, :Q_LORA + KV_LORA], kr, kr, kr_sw, kr_sw], axis=1)
    uq = w_uq.reshape(Q_LORA, MLA_HEADS, QK_NOPE + QK_ROPE)
    q_nope = uq[:, :, :QK_NOPE].reshape(Q_LORA, -1)
    q_rope = uq[:, :, QK_NOPE:].reshape(Q_LORA, -1)
    w_uq2 = jnp.concatenate([q_nope, q_rope, _swap_halves(q_rope)], axis=1)
    ukv = w_ukv.reshape(KV_LORA, MLA_HEADS, QK_NOPE + V_DIM)
    w_ukv2 = jnp.concatenate([ukv[:, :, :QK_NOPE].reshape(KV_LORA, -1),
                              ukv[:, :, QK_NOPE:].reshape(KV_LORA, -1)], axis=1)
    return w_in2.astype(BF16), w_uq2.astype(BF16), w_ukv2.astype(BF16)


def _pad_lanes(v):
    return jnp.pad(v, (0, LANES - v.shape[0]))[None, :]


def kernel(x, positions, mix_norm, ffn_norm, final_norm, mla_w_in, mla_q_norm, mla_kv_norm, mla_w_uq,
           mla_w_ukv, mla_w_o, ssm_w_in, ssm_conv_w, ssm_conv_b, ssm_dt_bias, ssm_A_log, ssm_D, ssm_norm,
           ssm_w_out, ffn_w_gate, ffn_w_up, ffn_w_down):
    b, s, d = x.shape
    n = b * s
    cos, sin = _rope_tables(positions)
    tril = jnp.tril(jnp.ones((SSM_CHUNK, SSM_CHUNK), BF16))
    final_g = final_norm[None, :]

    h = x.reshape(n, d)
    for i in range(DEPTH):
        k = i // N_MIXERS
        g_mix = mix_norm[i][None, :]
        if i % N_MIXERS == 0:
            w_in, w_uq, w_ukv = _mla_weights(mla_w_in[k], mla_w_uq[k], mla_w_ukv[k])
            q, kk, v = _mla_proj(h, g_mix, w_in, mla_q_norm[k][None, :], mla_kv_norm[k][None, :],
                                 w_uq, w_ukv, cos, sin)
            o = _attention(q.reshape(b, s, -1), kk.reshape(b, s, -1), v.reshape(b, s, -1))
            h = _proj_residual(h, o.reshape(n, -1), mla_w_o[k].astype(BF16))
        else:
            w = ssm_w_in[k]
            wz = w[:, :SSM_INNER].astype(BF16)
            wx = w[:, SSM_INNER:SSM_INNER + SSM_CONV_DIM].astype(BF16)
            wdt = jnp.pad(w[:, SSM_INNER + SSM_CONV_DIM:], ((0, 0), (0, LANES - SSM_HEADS))).astype(BF16)
            z, xbc, dt = _ssm_in(h, g_mix, wz, wx, wdt)
            y = _ssd(xbc.reshape(b, s, -1), z.reshape(b, s, -1), dt.reshape(b, s, -1),
                     ssm_conv_w[k], ssm_conv_b[k][None, :], _pad_lanes(ssm_dt_bias[k]),
                     _pad_lanes(ssm_A_log[k]), jnp.repeat(ssm_D[k], SSM_HEADDIM)[None, :],
                     ssm_norm[k][None, :], tril)
            h = _proj_residual(h, y.reshape(n, -1), ssm_w_out[k].astype(BF16))
        h = _ffn(h, ffn_norm[i][None, :], ffn_w_gate[i].astype(BF16), ffn_w_up[i].astype(BF16),
                 ffn_w_down[i].astype(BF16), final_g, final_norm=(i == DEPTH - 1))
    return h.reshape(b, s, d)
```

```python
import functools
import math

import jax
import jax.numpy as jnp
from jax import lax
from jax.experimental import pallas as pl
from jax.experimental.pallas import tpu as pltpu

F32 = jnp.float32
BF16 = jnp.bfloat16

D_MODEL = 1024
DEPTH = 4
N_MIXERS = 2

MLA_HEADS = 8
Q_LORA = 512
KV_LORA = 256
QK_NOPE = 128
QK_ROPE = 64
V_DIM = 128
ROPE_THETA = 10000.0
ROPE_HALF = QK_ROPE // 2
HEAD_PAIRS = MLA_HEADS // 2
Q_PAIR_W = 2 * QK_NOPE + 2 * QK_ROPE
K_PAIR_W = 2 * QK_NOPE + 4 * QK_ROPE
QK_CAT = QK_NOPE + 2 * QK_ROPE
LOG2_E = math.log2(math.e)
Q_SCALE = (QK_NOPE + QK_ROPE) ** -0.5 * LOG2_E

SSM_INNER = 2 * D_MODEL
SSM_HEADDIM = 64
SSM_HEADS = SSM_INNER // SSM_HEADDIM
SSM_GROUPS = 8
SSM_HPG = SSM_HEADS // SSM_GROUPS
SSM_STATE = 128
SSM_CONV = 4
SSM_CHUNK = 128
SSM_BC_DIM = SSM_GROUPS * SSM_STATE
SSM_CONV_DIM = SSM_INNER + 2 * SSM_BC_DIM
SSM_GROUP_W = SSM_HPG * SSM_HEADDIM
SSM_NORM_EPS = 1e-5

FFN_HIDDEN = 2816
NORM_EPS = 1e-6

LANES = 128
SUBLANES = 8
VMEM_LIMIT_BYTES = 56 * 1024 * 1024

TOKEN_TILE = 512
ATTN_TILE = 256
FFN_CHUNKS = (512, 512, 512, 512, 512, 256)
CONV_HIST = SUBLANES


def _params(*sem, flags=None):
    return pltpu.CompilerParams(dimension_semantics=sem, vmem_limit_bytes=VMEM_LIMIT_BYTES, flags=flags)


def _resident(shape):
    return pl.BlockSpec(shape, lambda *_: (0,) * len(shape), pipeline_mode=pl.Buffered(1))


def _rms(x, g, eps):
    return x * lax.rsqrt(jnp.mean(x * x, axis=-1, keepdims=True) + eps) * g


def _silu(x):
    return x * jax.nn.sigmoid(x)


def _dot(a, b):
    return jnp.dot(a, b, preferred_element_type=F32)


def _dot_nt(a, b):
    return lax.dot_general(a, b, (((1,), (1,)), ((), ())), preferred_element_type=F32)


def _rope_kernel(pos_ref, inv_ref, sgn_ref, cos_ref, sin_ref):
    ang = pos_ref[...].astype(F32) * inv_ref[...]
    cos_ref[...] = jnp.cos(ang)
    sin_ref[...] = jnp.sin(ang) * sgn_ref[...]


def _rope_tables(positions):
    n = positions.size
    tm = min(n, 2048)
    inv = jnp.power(ROPE_THETA, -jnp.arange(ROPE_HALF, dtype=F32) / ROPE_HALF)
    inv = jnp.tile(inv, LANES // ROPE_HALF)[None, :]
    sgn = jnp.tile(jnp.concatenate([-jnp.ones(ROPE_HALF, F32), jnp.ones(ROPE_HALF, F32)]),
                   LANES // QK_ROPE)[None, :]
    row = pl.BlockSpec((tm, LANES), lambda i: (i, 0))
    return pl.pallas_call(
        _rope_kernel,
        grid=(n // tm,),
        in_specs=[pl.BlockSpec((tm, 1), lambda i: (i, 0)), _resident((1, LANES)), _resident((1, LANES))],
        out_specs=[row, row],
        out_shape=[jax.ShapeDtypeStruct((n, LANES), F32)] * 2,
        compiler_params=_params("parallel"),
        name="rope_tables",
    )(positions.reshape(n, 1), inv, sgn)


def _mla_proj_kernel(x_ref, g_ref, win_ref, qn_ref, kvn_ref, wuq_ref, wukv_ref, cos_ref, sin_ref,
                     q_ref, k_ref, v_ref):
    u = _rms(x_ref[...], g_ref[...], NORM_EPS).astype(BF16)
    lat = _dot(u, win_ref[...])
    cos = cos_ref[...]
    sin = sin_ref[...]
    kv0 = Q_LORA
    kr0 = Q_LORA + KV_LORA
    qn = _rms(lat[:, :Q_LORA], qn_ref[...], NORM_EPS).astype(BF16)
    kvn = _rms(lat[:, kv0:kr0], kvn_ref[...], NORM_EPS).astype(BF16)
    q = _dot(qn, wuq_ref[...])
    kv = _dot(kvn, wukv_ref[...])

    kr = lat[:, kr0:kr0 + LANES] * cos + lat[:, kr0 + LANES:kr0 + 2 * LANES] * sin
    lane = lax.broadcasted_iota(jnp.int32, kr.shape, 1)
    kr_lo = jnp.where(lane < QK_ROPE, kr, 0.0).astype(BF16)
    kr_hi = jnp.where(lane >= QK_ROPE, kr, 0.0).astype(BF16)

    nope_w = MLA_HEADS * QK_NOPE
    rope_w = MLA_HEADS * QK_ROPE
    for p in range(HEAD_PAIRS):
        a = 2 * p * QK_NOPE
        b = a + QK_NOPE
        r = nope_w + p * LANES
        qr = q[:, r:r + LANES] * cos + q[:, r + rope_w:r + rope_w + LANES] * sin
        q0 = p * Q_PAIR_W
        q_ref[:, q0:q0 + LANES] = (q[:, a:a + QK_NOPE] * Q_SCALE).astype(BF16)
        q_ref[:, q0 + LANES:q0 + 2 * LANES] = (qr * Q_SCALE).astype(BF16)
        q_ref[:, q0 + 2 * LANES:q0 + 3 * LANES] = (q[:, b:b + QK_NOPE] * Q_SCALE).astype(BF16)
        k0 = p * K_PAIR_W
        k_ref[:, k0:k0 + LANES] = kv[:, a:a + QK_NOPE].astype(BF16)
        k_ref[:, k0 + LANES:k0 + 2 * LANES] = kr_lo
        k_ref[:, k0 + 2 * LANES:k0 + 3 * LANES] = kr_hi
        k_ref[:, k0 + 3 * LANES:k0 + 4 * LANES] = kv[:, b:b + QK_NOPE].astype(BF16)
    v_ref[...] = kv[:, nope_w:].astype(BF16)


def _mla_proj(h, g, w_in, q_norm, kv_norm, w_uq, w_ukv, cos, sin):
    n = h.shape[0]
    tm = min(n, TOKEN_TILE)
    qw, kw, vw = HEAD_PAIRS * Q_PAIR_W, HEAD_PAIRS * K_PAIR_W, MLA_HEADS * V_DIM
    row = lambda w: pl.BlockSpec((tm, w), lambda i: (i, 0))
    return pl.pallas_call(
        _mla_proj_kernel,
        grid=(n // tm,),
        in_specs=[row(D_MODEL), _resident(g.shape), _resident(w_in.shape), _resident(q_norm.shape),
                  _resident(kv_norm.shape), _resident(w_uq.shape), _resident(w_ukv.shape),
                  row(LANES), row(LANES)],
        out_specs=[row(qw), row(kw), row(vw)],
        out_shape=[jax.ShapeDtypeStruct((n, qw), BF16), jax.ShapeDtypeStruct((n, kw), BF16),
                   jax.ShapeDtypeStruct((n, vw), BF16)],
        compiler_params=_params("parallel"),
        name="mla_proj",
    )(h, g, w_in, q_norm, kv_norm, w_uq, w_ukv, cos, sin)


def _attn_kernel(q_ref, k_ref, v_ref, o_ref):
    t = ATTN_TILE
    rows = lax.broadcasted_iota(jnp.int32, (t, t), 0)
    cols = lax.broadcasted_iota(jnp.int32, (t, t), 1)
    causal = cols <= rows

    for i in range(q_ref.shape[0] // t):
        r0 = i * t
        for hh in range(2):
            qh = q_ref[r0:r0 + t, hh * LANES:hh * LANES + QK_CAT]
            kc = slice(hh * QK_CAT, (hh + 1) * QK_CAT)
            vc = slice(hh * V_DIM, (hh + 1) * V_DIM)
            s_d = jnp.where(causal, _dot_nt(qh, k_ref[r0:r0 + t, kc]), -jnp.inf)
            m = jnp.max(s_d, axis=-1, keepdims=True)
            if i:
                s_p = _dot_nt(qh, k_ref[0:r0, kc])
                m = jnp.maximum(m, jnp.max(s_p, axis=-1, keepdims=True))
            p_d = jnp.exp2(s_d - m)
            l = jnp.sum(p_d, axis=-1, keepdims=True)
            o = _dot(p_d.astype(BF16), v_ref[r0:r0 + t, vc])
            if i:
                p_p = jnp.exp2(s_p - m)
                l = l + jnp.sum(p_p, axis=-1, keepdims=True)
                o = o + _dot(p_p.astype(BF16), v_ref[0:r0, vc])
            o_ref[r0:r0 + t, vc] = (o / l).astype(BF16)


def _attention(q, k, v):
    b, s, _ = q.shape
    blk = lambda w: pl.BlockSpec((None, s, w), lambda bi, p: (bi, 0, p))
    return pl.pallas_call(
        _attn_kernel,
        grid=(b, HEAD_PAIRS),
        in_specs=[blk(Q_PAIR_W), blk(K_PAIR_W), blk(2 * V_DIM)],
        out_specs=blk(2 * V_DIM),
        out_shape=jax.ShapeDtypeStruct((b, s, MLA_HEADS * V_DIM), BF16),
        compiler_params=_params("parallel", "parallel"),
        name="mla_attention",
    )(q, k, v)


def _ffn_kernel(*refs, final_norm):
    h_ref, a_ref, wp_ref, g_ref, wg_ref, wu_ref, wd_ref = refs[:7]
    o_ref, act_ref = refs[-2:]
    x = h_ref[...] + _dot(a_ref[...], wp_ref[...])
    u = _rms(x, g_ref[...], NORM_EPS).astype(BF16)
    off = 0
    for c in FFN_CHUNKS:
        gate = _dot(u, wg_ref[:, off:off + c])
        up = _dot(u, wu_ref[:, off:off + c])
        act_ref[:, off:off + c] = (_silu(gate) * up).astype(BF16)
        off += c
    y = x + _dot(act_ref[...], wd_ref[...])
    if final_norm:
        y = _rms(y, refs[7][...], NORM_EPS)
    o_ref[...] = y


def _proj_ffn(h, a, w_proj, g, w_gate, w_up, w_down, final_g=None):
    n, kdim = a.shape
    tm = min(n, TOKEN_TILE)
    row = lambda w: pl.BlockSpec((tm, w), lambda i: (i, 0))
    final_norm = final_g is not None
    args = [h, a, w_proj, g, w_gate, w_up, w_down] + ([final_g] if final_norm else [])
    return pl.pallas_call(
        functools.partial(_ffn_kernel, final_norm=final_norm),
        grid=(n // tm,),
        in_specs=[row(D_MODEL), row(kdim)] + [_resident(w.shape) for w in args[2:]],
        out_specs=row(D_MODEL),
        out_shape=jax.ShapeDtypeStruct((n, D_MODEL), F32),
        scratch_shapes=[pltpu.VMEM((tm, FFN_HIDDEN), BF16)],
        compiler_params=_params("parallel"),
        name="proj_ffn_final" if final_norm else "proj_ffn",
    )(*args)


def _ssm_in_kernel(x_ref, g_ref, wz_ref, wx_ref, wdt_ref, z_ref, xbc_ref, dt_ref):
    u = _rms(x_ref[...], g_ref[...], NORM_EPS).astype(BF16)
    z_ref[...] = _silu(_dot(u, wz_ref[...])).astype(BF16)
    step = SSM_CONV_DIM // 4
    for c in range(0, SSM_CONV_DIM, step):
        xbc_ref[:, c:c + step] = _dot(u, wx_ref[:, c:c + step]).astype(BF16)
    dt_ref[...] = _dot(u, wdt_ref[...])


def _ssm_in(h, g, wz, wx, wdt):
    n = h.shape[0]
    tm = min(n, TOKEN_TILE)
    row = lambda w: pl.BlockSpec((tm, w), lambda i: (i, 0))
    return pl.pallas_call(
        _ssm_in_kernel,
        grid=(n // tm,),
        in_specs=[row(D_MODEL), _resident(g.shape), _resident(wz.shape), _resident(wx.shape),
                  _resident(wdt.shape)],
        out_specs=[row(SSM_INNER), row(SSM_CONV_DIM), row(LANES)],
        out_shape=[jax.ShapeDtypeStruct((n, SSM_INNER), BF16), jax.ShapeDtypeStruct((n, SSM_CONV_DIM), BF16),
                   jax.ShapeDtypeStruct((n, LANES), F32)],
        compiler_params=_params("parallel"),
        name="ssm_in",
    )(h, g, wz, wx, wdt)


def _split3(x):
    hi = x.astype(BF16)
    r = x - hi.astype(F32)
    mid = r.astype(BF16)
    lo = (r - mid.astype(F32)).astype(BF16)
    return hi, mid, lo


def _ssd_kernel(xbc_ref, z_ref, dt_ref, cw_ref, cb_ref, dtb_ref, alog_ref, dskip_ref, nw_ref,
                tril_ref, expand_ref, y_ref, state_ref, ext_ref):
    L = SSM_CHUNK
    H0 = CONV_HIST

    @pl.when(pl.program_id(1) == 0)
    def _():
        state_ref[...] = jnp.zeros_like(state_ref)
        ext_ref[0:H0, :] = jnp.zeros((H0, SSM_CONV_DIM), F32)

    ext_ref[H0:H0 + L, :] = xbc_ref[...].astype(F32)

    def conv_silu(c0, w):
        acc = cb_ref[:, c0:c0 + w]
        for k in range(SSM_CONV):
            r0 = H0 - (SSM_CONV - 1) + k
            acc = acc + cw_ref[k:k + 1, c0:c0 + w] * ext_ref[r0:r0 + L, c0:c0 + w]
        return _silu(acc)

    dtv = dt_ref[...] + dtb_ref[...]
    dt = jnp.maximum(dtv, 0.0) + jnp.log1p(jnp.exp(-jnp.abs(dtv)))
    da = dt * (-jnp.exp(alog_ref[...]))
    tril = tril_ref[...]
    hi, mid, lo = _split3(da)
    cum = _dot(tril, hi) + _dot(tril, mid) + _dot(tril, lo)
    cum_t = cum.T
    dt_t = dt.T
    last_t = cum_t[:, L - 1:L]
    ws16 = (jnp.exp(last_t - cum_t) * dt_t).astype(BF16)
    elast_t = jnp.exp(last_t)
    cum2 = cum * LOG2_E
    cum2_t = cum_t * LOG2_E
    col2 = _dot(jnp.concatenate(_split3(cum2), axis=1), expand_ref[...])

    rows = lax.broadcasted_iota(jnp.int32, (L, L), 0)
    cols = lax.broadcasted_iota(jnp.int32, (L, L), 1)
    causal = rows >= cols
    first = lax.broadcasted_iota(jnp.int32, (L, LANES), 1) < SSM_HEADDIM

    for g in range(SSM_GROUPS):
        x0 = g * SSM_GROUP_W
        b0 = SSM_INNER + g * SSM_STATE
        b16 = conv_silu(b0, SSM_STATE).astype(BF16)
        c16 = conv_silu(b0 + SSM_BC_DIM, SSM_STATE).astype(BF16)
        cb = _dot_nt(c16, b16)
        b_t16 = b16.astype(F32).T.astype(BF16)
        y_in = _dot(c16, state_ref[g].astype(BF16))
        gated = []
        for pr in range(SSM_HPG // 2):
            c0 = x0 + pr * LANES
            sl = slice(pr * LANES, (pr + 1) * LANES)
            xp = conv_silu(c0, LANES)
            xp16 = xp.astype(BF16)
            zero = jnp.zeros_like(xp16)
            xbd = jnp.concatenate([jnp.where(first, xp16, zero), jnp.where(first, zero, xp16)], axis=0)
            w, bs, e_col, e_last = [], [], [], []
            for j in range(2):
                hd = g * SSM_HPG + 2 * pr + j
                col = col2[:, hd * L:(hd + 1) * L]
                row = cum2_t[hd:hd + 1, :]
                decay = jnp.exp2(jnp.where(causal, col - row, -jnp.inf))
                w.append((cb * decay * dt_t[hd:hd + 1, :]).astype(BF16))
                bs.append(b_t16 * ws16[hd:hd + 1, :])
                e_col.append(jnp.exp2(col))
                e_last.append(elast_t[hd:hd + 1, :])
            y = (_dot(jnp.concatenate(w, axis=1), xbd)
                 + y_in[:, sl] * jnp.where(first, e_col[0], e_col[1])
                 + dskip_ref[:, c0:c0 + LANES] * xp)
            state_ref[g, :, sl] = (state_ref[g, :, sl] * jnp.where(first, e_last[0], e_last[1])
                                   + _dot(jnp.concatenate(bs, axis=1), xbd))
            gated.append(y * z_ref[:, c0:c0 + LANES].astype(F32))
        gated = jnp.concatenate(gated, axis=1)
        ms = jnp.mean(gated * gated, axis=-1, keepdims=True)
        y_ref[:, x0:x0 + SSM_GROUP_W] = (gated * lax.rsqrt(ms + SSM_NORM_EPS)
                                         * nw_ref[:, x0:x0 + SSM_GROUP_W]).astype(y_ref.dtype)

    ext_ref[0:H0, :] = ext_ref[L:L + H0, :]


def _ssd(xbc, z, dt, conv_w, conv_b, dt_bias, a_log, d_skip, norm_w, tril, expand):
    b, s, _ = xbc.shape
    L = SSM_CHUNK
    blk = lambda w: pl.BlockSpec((None, L, w), lambda bi, c: (bi, c, 0))
    consts = [conv_w, conv_b, dt_bias, a_log, d_skip, norm_w, tril, expand]
    return pl.pallas_call(
        _ssd_kernel,
        grid=(b, s // L),
        in_specs=[blk(SSM_CONV_DIM), blk(SSM_INNER), blk(LANES)] + [_resident(c.shape) for c in consts],
        out_specs=blk(SSM_INNER),
        out_shape=jax.ShapeDtypeStruct((b, s, SSM_INNER), BF16),
        scratch_shapes=[pltpu.VMEM((SSM_GROUPS, SSM_STATE, SSM_GROUP_W), F32),
                        pltpu.VMEM((CONV_HIST + L, SSM_CONV_DIM), F32)],
        compiler_params=_params("parallel", "arbitrary"),
        name="ssd",
    )(xbc, z, dt, *consts)


def _swap_halves(w):
    lead = w.shape[:-1]
    w = w.reshape(*lead, -1, 2, ROPE_HALF)
    return jnp.flip(w, axis=-2).reshape(*lead, -1)


def _mla_weights(w_in, w_uq, w_ukv):
    kr = w_in[:, Q_LORA + KV_LORA:]
    kr_sw = _swap_halves(kr)
    w_in2 = jnp.concatenate([w_in[:, :Q_LORA + KV_LORA], kr, kr, kr_sw, kr_sw], axis=1)
    uq = w_uq.reshape(Q_LORA, MLA_HEADS, QK_NOPE + QK_ROPE)
    q_nope = uq[:, :, :QK_NOPE].reshape(Q_LORA, -1)
    q_rope = uq[:, :, QK_NOPE:].reshape(Q_LORA, -1)
    w_uq2 = jnp.concatenate([q_nope, q_rope, _swap_halves(q_rope)], axis=1)
    ukv = w_ukv.reshape(KV_LORA, MLA_HEADS, QK_NOPE + V_DIM)
    w_ukv2 = jnp.concatenate([ukv[:, :, :QK_NOPE].reshape(KV_LORA, -1),
                              ukv[:, :, QK_NOPE:].reshape(KV_LORA, -1)], axis=1)
    return w_in2.astype(BF16), w_uq2.astype(BF16), w_ukv2.astype(BF16)


def _pad_lanes(v):
    return jnp.pad(v, (0, LANES - v.shape[0]))[None, :]


def kernel(x, positions, mix_norm, ffn_norm, final_norm, mla_w_in, mla_q_norm, mla_kv_norm, mla_w_uq,
           mla_w_ukv, mla_w_o, ssm_w_in, ssm_conv_w, ssm_conv_b, ssm_dt_bias, ssm_A_log, ssm_D, ssm_norm,
           ssm_w_out, ffn_w_gate, ffn_w_up, ffn_w_down):
    b, s, d = x.shape
    n = b * s
    cos, sin = _rope_tables(positions)
    tril = jnp.tril(jnp.ones((SSM_CHUNK, SSM_CHUNK), BF16))
    expand = jnp.tile(jnp.repeat(jnp.eye(LANES, SSM_HEADS, dtype=BF16), SSM_CHUNK, axis=1), (3, 1))

    h = x.reshape(n, d)
    for i in range(DEPTH):
        k = i // N_MIXERS
        g_mix = mix_norm[i][None, :]
        if i % N_MIXERS == 0:
            w_in, w_uq, w_ukv = _mla_weights(mla_w_in[k], mla_w_uq[k], mla_w_ukv[k])
            q, kk, v = _mla_proj(h, g_mix, w_in, mla_q_norm[k][None, :], mla_kv_norm[k][None, :],
                                 w_uq, w_ukv, cos, sin)
            a = _attention(q.reshape(b, s, -1), kk.reshape(b, s, -1), v.reshape(b, s, -1))
            w_proj = mla_w_o[k]
        else:
            w = ssm_w_in[k]
            wz = w[:, :SSM_INNER].astype(BF16)
            wx = w[:, SSM_INNER:SSM_INNER + SSM_CONV_DIM].astype(BF16)
            wdt = jnp.pad(w[:, SSM_INNER + SSM_CONV_DIM:], ((0, 0), (0, LANES - SSM_HEADS))).astype(BF16)
            z, xbc, dt = _ssm_in(h, g_mix, wz, wx, wdt)
            a = _ssd(xbc.reshape(b, s, -1), z.reshape(b, s, -1), dt.reshape(b, s, -1), ssm_conv_w[k],
                     ssm_conv_b[k][None, :], _pad_lanes(ssm_dt_bias[k]), _pad_lanes(ssm_A_log[k]),
                     jnp.repeat(ssm_D[k], SSM_HEADDIM)[None, :], ssm_norm[k][None, :], tril, expand)
            w_proj = ssm_w_out[k]
        h = _proj_ffn(h, a.reshape(n, -1), w_proj.astype(BF16), ffn_norm[i][None, :],
                      ffn_w_gate[i].astype(BF16), ffn_w_up[i].astype(BF16), ffn_w_down[i].astype(BF16),
                      final_norm[None, :] if i == DEPTH - 1 else None)
    return h.reshape(b, s, d)
```

```python
import functools
import math

import jax
import jax.numpy as jnp
from jax import lax
from jax.experimental import pallas as pl
from jax.experimental.pallas import tpu as pltpu

F32 = jnp.float32
BF16 = jnp.bfloat16

D_MODEL = 1024
DEPTH = 4
N_MIXERS = 2

MLA_HEADS = 8
Q_LORA = 512
KV_LORA = 256
QK_NOPE = 128
QK_ROPE = 64
V_DIM = 128
ROPE_THETA = 10000.0
ROPE_HALF = QK_ROPE // 2
HEAD_PAIRS = MLA_HEADS // 2
Q_PAIR_W = 2 * QK_NOPE + 2 * QK_ROPE
K_PAIR_W = 2 * QK_NOPE + 4 * QK_ROPE
QK_CAT = QK_NOPE + 2 * QK_ROPE
LOG2_E = math.log2(math.e)
Q_SCALE = (QK_NOPE + QK_ROPE) ** -0.5 * LOG2_E

SSM_INNER = 2 * D_MODEL
SSM_HEADDIM = 64
SSM_HEADS = SSM_INNER // SSM_HEADDIM
SSM_GROUPS = 8
SSM_HPG = SSM_HEADS // SSM_GROUPS
SSM_STATE = 128
SSM_CONV = 4
SSM_CHUNK = 128
SSM_BC_DIM = SSM_GROUPS * SSM_STATE
SSM_CONV_DIM = SSM_INNER + 2 * SSM_BC_DIM
SSM_GROUP_W = SSM_HPG * SSM_HEADDIM
SSM_NORM_EPS = 1e-5

FFN_HIDDEN = 2816
NORM_EPS = 1e-6

LANES = 128
SUBLANES = 8
VMEM_LIMIT_BYTES = 56 * 1024 * 1024

TOKEN_TILE = 512
ATTN_TILE = 256
FFN_CHUNKS = (512, 512, 512, 512, 512, 256)
CONV_HIST = 2 * SUBLANES
CONV_COLS = 512


def _params(*sem, flags=None):
    return pltpu.CompilerParams(dimension_semantics=sem, vmem_limit_bytes=VMEM_LIMIT_BYTES, flags=flags)


def _resident(shape):
    return pl.BlockSpec(shape, lambda *_: (0,) * len(shape), pipeline_mode=pl.Buffered(1))


def _resident_layer(stack, k):
    return pl.BlockSpec((None,) + stack.shape[1:], lambda *_: (k, 0, 0), pipeline_mode=pl.Buffered(1))


def _rms(x, g, eps):
    return x * lax.rsqrt(jnp.mean(x * x, axis=-1, keepdims=True) + eps) * g


def _silu(x):
    return x * jax.nn.sigmoid(x)


def _dot(a, b):
    return jnp.dot(a, b, preferred_element_type=F32)


def _dot_nt(a, b):
    return lax.dot_general(a, b, (((1,), (1,)), ((), ())), preferred_element_type=F32)


def _rope_kernel(pos_ref, inv_ref, sgn_ref, cos_ref, sin_ref):
    ang = pos_ref[...].astype(F32) * inv_ref[...]
    cos_ref[...] = jnp.cos(ang)
    sin_ref[...] = jnp.sin(ang) * sgn_ref[...]


def _rope_tables(positions):
    n = positions.size
    tm = min(n, 2048)
    inv = jnp.power(ROPE_THETA, -jnp.arange(ROPE_HALF, dtype=F32) / ROPE_HALF)
    inv = jnp.tile(inv, LANES // ROPE_HALF)[None, :]
    sgn = jnp.tile(jnp.concatenate([-jnp.ones(ROPE_HALF, F32), jnp.ones(ROPE_HALF, F32)]),
                   LANES // QK_ROPE)[None, :]
    row = pl.BlockSpec((tm, LANES), lambda i: (i, 0))
    return pl.pallas_call(
        _rope_kernel,
        grid=(n // tm,),
        in_specs=[pl.BlockSpec((tm, 1), lambda i: (i, 0)), _resident((1, LANES)), _resident((1, LANES))],
        out_specs=[row, row],
        out_shape=[jax.ShapeDtypeStruct((n, LANES), F32)] * 2,
        compiler_params=_params("parallel"),
        name="rope_tables",
    )(positions.reshape(n, 1), inv, sgn)


def _mla_proj_kernel(x_ref, g_ref, win_ref, qn_ref, kvn_ref, wuq_ref, wukv_ref, cos_ref, sin_ref,
                     q_ref, k_ref, v_ref):
    u = _rms(x_ref[...], g_ref[...], NORM_EPS).astype(BF16)
    lat = _dot(u, win_ref[...])
    cos = cos_ref[...]
    sin = sin_ref[...]
    kv0 = Q_LORA
    kr0 = Q_LORA + KV_LORA
    qn = _rms(lat[:, :Q_LORA], qn_ref[...], NORM_EPS).astype(BF16)
    kvn = _rms(lat[:, kv0:kr0], kvn_ref[...], NORM_EPS).astype(BF16)
    q = _dot(qn, wuq_ref[...])
    kv = _dot(kvn, wukv_ref[...])

    kr = lat[:, kr0:kr0 + LANES] * cos + lat[:, kr0 + LANES:kr0 + 2 * LANES] * sin
    lane = lax.broadcasted_iota(jnp.int32, kr.shape, 1)
    kr_lo = jnp.where(lane < QK_ROPE, kr, 0.0).astype(BF16)
    kr_hi = jnp.where(lane >= QK_ROPE, kr, 0.0).astype(BF16)

    nope_w = MLA_HEADS * QK_NOPE
    rope_w = MLA_HEADS * QK_ROPE
    for p in range(HEAD_PAIRS):
        a = 2 * p * QK_NOPE
        b = a + QK_NOPE
        r = nope_w + p * LANES
        qr = q[:, r:r + LANES] * cos + q[:, r + rope_w:r + rope_w + LANES] * sin
        q0 = p * Q_PAIR_W
        q_ref[:, q0:q0 + LANES] = (q[:, a:a + QK_NOPE] * Q_SCALE).astype(BF16)
        q_ref[:, q0 + LANES:q0 + 2 * LANES] = (qr * Q_SCALE).astype(BF16)
        q_ref[:, q0 + 2 * LANES:q0 + 3 * LANES] = (q[:, b:b + QK_NOPE] * Q_SCALE).astype(BF16)
        k0 = p * K_PAIR_W
        k_ref[:, k0:k0 + LANES] = kv[:, a:a + QK_NOPE].astype(BF16)
        k_ref[:, k0 + LANES:k0 + 2 * LANES] = kr_lo
        k_ref[:, k0 + 2 * LANES:k0 + 3 * LANES] = kr_hi
        k_ref[:, k0 + 3 * LANES:k0 + 4 * LANES] = kv[:, b:b + QK_NOPE].astype(BF16)
    v_ref[...] = kv[:, nope_w:].astype(BF16)


def _mla_proj(h, g, w_in, q_norm, kv_norm, w_uq, w_ukv, cos, sin):
    n = h.shape[0]
    tm = min(n, TOKEN_TILE)
    qw, kw, vw = HEAD_PAIRS * Q_PAIR_W, HEAD_PAIRS * K_PAIR_W, MLA_HEADS * V_DIM
    row = lambda w: pl.BlockSpec((tm, w), lambda i: (i, 0))
    return pl.pallas_call(
        _mla_proj_kernel,
        grid=(n // tm,),
        in_specs=[row(D_MODEL), _resident(g.shape), _resident(w_in.shape), _resident(q_norm.shape),
                  _resident(kv_norm.shape), _resident(w_uq.shape), _resident(w_ukv.shape),
                  row(LANES), row(LANES)],
        out_specs=[row(qw), row(kw), row(vw)],
        out_shape=[jax.ShapeDtypeStruct((n, qw), BF16), jax.ShapeDtypeStruct((n, kw), BF16),
                   jax.ShapeDtypeStruct((n, vw), BF16)],
        compiler_params=_params("parallel"),
        name="mla_proj",
    )(h, g, w_in, q_norm, kv_norm, w_uq, w_ukv, cos, sin)


def _attn_kernel(q_ref, k_ref, v_ref, o_ref):
    t = ATTN_TILE
    rows = lax.broadcasted_iota(jnp.int32, (t, t), 0)
    cols = lax.broadcasted_iota(jnp.int32, (t, t), 1)
    causal = cols <= rows

    for i in range(q_ref.shape[0] // t):
        r0 = i * t
        for hh in range(2):
            qh = q_ref[r0:r0 + t, hh * LANES:hh * LANES + QK_CAT]
            kc = slice(hh * QK_CAT, (hh + 1) * QK_CAT)
            vc = slice(hh * V_DIM, (hh + 1) * V_DIM)
            s = _dot_nt(qh, k_ref[0:r0 + t, kc])
            s_d = jnp.where(causal, s[:, r0:], -jnp.inf)
            m = jnp.max(s_d, axis=-1, keepdims=True)
            if i:
                m = jnp.maximum(m, jnp.max(s[:, :r0], axis=-1, keepdims=True))
            p_d = jnp.exp2(s_d - m)
            l = jnp.sum(p_d, axis=-1, keepdims=True)
            p = p_d.astype(BF16)
            if i:
                p_p = jnp.exp2(s[:, :r0] - m)
                l = l + jnp.sum(p_p, axis=-1, keepdims=True)
                p = jnp.concatenate([p_p.astype(BF16), p], axis=1)
            o_ref[r0:r0 + t, vc] = (_dot(p, v_ref[0:r0 + t, vc]) / l).astype(BF16)


def _attention(q, k, v):
    b, s, _ = q.shape
    blk = lambda w: pl.BlockSpec((None, s, w), lambda bi, p: (bi, 0, p))
    return pl.pallas_call(
        _attn_kernel,
        grid=(b, HEAD_PAIRS),
        in_specs=[blk(Q_PAIR_W), blk(K_PAIR_W), blk(2 * V_DIM)],
        out_specs=blk(2 * V_DIM),
        out_shape=jax.ShapeDtypeStruct((b, s, MLA_HEADS * V_DIM), BF16),
        compiler_params=_params("parallel", "parallel"),
        name="mla_attention",
    )(q, k, v)


def _ffn_kernel(*refs, final_norm):
    h_ref, a_ref, wp_ref, g_ref, wg_ref, wu_ref, wd_ref = refs[:7]
    o_ref, act_ref = refs[-2:]
    x = h_ref[...] + _dot(a_ref[...], wp_ref[...])
    u = _rms(x, g_ref[...], NORM_EPS).astype(BF16)
    off = 0
    for c in FFN_CHUNKS:
        gate = _dot(u, wg_ref[:, off:off + c])
        up = _dot(u, wu_ref[:, off:off + c])
        act_ref[:, off:off + c] = (_silu(gate) * up).astype(BF16)
        off += c
    y = x + _dot(act_ref[...], wd_ref[...])
    if final_norm:
        y = _rms(y, refs[7][...], NORM_EPS)
    o_ref[...] = y


def _proj_ffn(h, a, w_proj, k, g, w_gate, w_up, w_down, i, final_g=None):
    n, kdim = a.shape
    tm = min(n, TOKEN_TILE)
    row = lambda w: pl.BlockSpec((tm, w), lambda t: (t, 0))
    final_norm = final_g is not None
    args = [h, a, w_proj, g, w_gate, w_up, w_down] + ([final_g] if final_norm else [])
    return pl.pallas_call(
        functools.partial(_ffn_kernel, final_norm=final_norm),
        grid=(n // tm,),
        in_specs=[row(D_MODEL), row(kdim), _resident_layer(w_proj, k), _resident(g.shape),
                  _resident_layer(w_gate, i), _resident_layer(w_up, i), _resident_layer(w_down, i)]
        + ([_resident(final_g.shape)] if final_norm else []),
        out_specs=row(D_MODEL),
        out_shape=jax.ShapeDtypeStruct((n, D_MODEL), F32),
        scratch_shapes=[pltpu.VMEM((tm, FFN_HIDDEN), BF16)],
        compiler_params=_params("parallel"),
        name="proj_ffn_final" if final_norm else "proj_ffn",
    )(*args)


def _ssm_in_kernel(x_ref, g_ref, w_ref, wdt_ref, z_ref, xbc_ref, dt_ref):
    u = _rms(x_ref[...], g_ref[...], NORM_EPS).astype(BF16)
    z_ref[...] = _silu(_dot(u, w_ref[:, :SSM_INNER])).astype(BF16)
    step = SSM_CONV_DIM // 4
    for c in range(0, SSM_CONV_DIM, step):
        xbc_ref[:, c:c + step] = _dot(u, w_ref[:, SSM_INNER + c:SSM_INNER + c + step]).astype(BF16)
    dt_ref[...] = _dot(u, wdt_ref[...])


def _ssm_in(h, g, w_in, k, wdt):
    n = h.shape[0]
    tm = min(n, TOKEN_TILE)
    row = lambda w: pl.BlockSpec((tm, w), lambda t: (t, 0))
    return pl.pallas_call(
        _ssm_in_kernel,
        grid=(n // tm,),
        in_specs=[row(D_MODEL), _resident(g.shape), _resident_layer(w_in, k), _resident(wdt.shape)],
        out_specs=[row(SSM_INNER), row(SSM_CONV_DIM), row(LANES)],
        out_shape=[jax.ShapeDtypeStruct((n, SSM_INNER), BF16), jax.ShapeDtypeStruct((n, SSM_CONV_DIM), BF16),
                   jax.ShapeDtypeStruct((n, LANES), F32)],
        compiler_params=_params("parallel"),
        name="ssm_in",
    )(h, g, w_in, wdt)


def _split3(x):
    hi = x.astype(BF16)
    r = x - hi.astype(F32)
    mid = r.astype(BF16)
    lo = (r - mid.astype(F32)).astype(BF16)
    return hi, mid, lo


def _ssd_kernel(xbc_ref, z_ref, dt_ref, cw_ref, cb_ref, shift_ref, dtb_ref, alog_ref, dskip_ref, nw_ref,
                tril_ref, expand_ref, y_ref, state_ref, hist_ref, act_ref):
    L = SSM_CHUNK

    @pl.when(pl.program_id(1) == 0)
    def _():
        state_ref[...] = jnp.zeros_like(state_ref)
        hist_ref[...] = jnp.zeros_like(hist_ref)

    shift = shift_ref[...]
    cw16 = cw_ref[0:SSM_CONV - 1, :].astype(BF16)
    for c0 in range(0, SSM_CONV_DIM, CONV_COLS):
        cs = slice(c0, c0 + CONV_COLS)
        cur = xbc_ref[:, cs]
        ext = jnp.concatenate([hist_ref[:, cs], cur], axis=0)
        taps = [ext * cw16[SSM_CONV - 1 - d:SSM_CONV - d, cs] for d in range(1, SSM_CONV)]
        acc = (_dot(shift, jnp.concatenate(taps, axis=0))
               + cur.astype(F32) * cw_ref[SSM_CONV - 1:SSM_CONV, cs] + cb_ref[:, cs])
        act_ref[:, cs] = _silu(acc).astype(BF16)
    hist_ref[...] = xbc_ref[L - CONV_HIST:L, :]

    dtv = dt_ref[...] + dtb_ref[...]
    dt = jnp.maximum(dtv, 0.0) + jnp.log1p(jnp.exp(-jnp.abs(dtv)))
    da = dt * (-jnp.exp(alog_ref[...]))
    tril = tril_ref[...]
    hi, mid, lo = _split3(da)
    cum = _dot(tril, hi) + _dot(tril, mid) + _dot(tril, lo)
    cum_t = cum.T
    dt_t = dt.T
    last_t = cum_t[:, L - 1:L]
    ws16 = (jnp.exp(last_t - cum_t) * dt_t).astype(BF16)
    elast_t = jnp.exp(last_t)
    cum2 = cum * LOG2_E
    cum2_t = cum_t * LOG2_E
    col2 = _dot(jnp.concatenate(_split3(cum2), axis=1), expand_ref[...])

    rows = lax.broadcasted_iota(jnp.int32, (L, L), 0)
    cols = lax.broadcasted_iota(jnp.int32, (L, L), 1)
    causal = rows >= cols
    first = lax.broadcasted_iota(jnp.int32, (L, LANES), 1) < SSM_HEADDIM

    for g in range(SSM_GROUPS):
        x0 = g * SSM_GROUP_W
        b0 = SSM_INNER + g * SSM_STATE
        b16 = act_ref[:, b0:b0 + SSM_STATE]
        c16 = act_ref[:, b0 + SSM_BC_DIM:b0 + SSM_BC_DIM + SSM_STATE]
        cb = _dot_nt(c16, b16)
        b_t16 = b16.astype(F32).T.astype(BF16)
        y_in = _dot(c16, state_ref[g].astype(BF16))
        gated = []
        for pr in range(SSM_HPG // 2):
            c0 = x0 + pr * LANES
            sl = slice(pr * LANES, (pr + 1) * LANES)
            xp16 = act_ref[:, c0:c0 + LANES]
            zero = jnp.zeros_like(xp16)
            xbd = jnp.concatenate([jnp.where(first, xp16, zero), jnp.where(first, zero, xp16)], axis=0)
            w, bs, e_col, e_last = [], [], [], []
            for j in range(2):
                hd = g * SSM_HPG + 2 * pr + j
                col = col2[:, hd * L:(hd + 1) * L]
                row = cum2_t[hd:hd + 1, :]
                decay = jnp.exp2(jnp.where(causal, col - row, -jnp.inf))
                w.append((cb * decay * dt_t[hd:hd + 1, :]).astype(BF16))
                bs.append(b_t16 * ws16[hd:hd + 1, :])
                e_col.append(jnp.exp2(col))
                e_last.append(elast_t[hd:hd + 1, :])
            y = (_dot(jnp.concatenate(w, axis=1), xbd)
                 + y_in[:, sl] * jnp.where(first, e_col[0], e_col[1])
                 + dskip_ref[:, c0:c0 + LANES] * xp16.astype(F32))
            state_ref[g, :, sl] = (state_ref[g, :, sl] * jnp.where(first, e_last[0], e_last[1])
                                   + _dot(jnp.concatenate(bs, axis=1), xbd))
            gated.append(y * z_ref[:, c0:c0 + LANES].astype(F32))
        gated = jnp.concatenate(gated, axis=1)
        ms = jnp.mean(gated * gated, axis=-1, keepdims=True)
        y_ref[:, x0:x0 + SSM_GROUP_W] = (gated * lax.rsqrt(ms + SSM_NORM_EPS)
                                         * nw_ref[:, x0:x0 + SSM_GROUP_W]).astype(y_ref.dtype)


def _ssd(xbc, z, dt, conv_w, conv_b, shift, dt_bias, a_log, d_skip, norm_w, tril, expand):
    b, s, _ = xbc.shape
    L = SSM_CHUNK
    blk = lambda w: pl.BlockSpec((None, L, w), lambda bi, c: (bi, c, 0))
    consts = [conv_w, conv_b, shift, dt_bias, a_log, d_skip, norm_w, tril, expand]
    return pl.pallas_call(
        _ssd_kernel,
        grid=(b, s // L),
        in_specs=[blk(SSM_CONV_DIM), blk(SSM_INNER), blk(LANES)] + [_resident(c.shape) for c in consts],
        out_specs=blk(SSM_INNER),
        out_shape=jax.ShapeDtypeStruct((b, s, SSM_INNER), BF16),
        scratch_shapes=[pltpu.VMEM((SSM_GROUPS, SSM_STATE, SSM_GROUP_W), F32),
                        pltpu.VMEM((CONV_HIST, SSM_CONV_DIM), BF16),
                        pltpu.VMEM((L, SSM_CONV_DIM), BF16)],
        compiler_params=_params("parallel", "arbitrary"),
        name="ssd",
    )(xbc, z, dt, *consts)


def _conv_shift_matrix():
    e = CONV_HIST + SSM_CHUNK
    t = jnp.arange(SSM_CHUNK)[:, None]
    j = jnp.arange(e)[None, :]
    blocks = [(j == t - d + CONV_HIST) for d in range(1, SSM_CONV)]
    return jnp.concatenate(blocks, axis=1).astype(BF16)


def _swap_halves(w):
    lead = w.shape[:-1]
    w = w.reshape(*lead, -1, 2, ROPE_HALF)
    return jnp.flip(w, axis=-2).reshape(*lead, -1)


def _mla_weights(w_in, w_uq, w_ukv):
    kr = w_in[:, Q_LORA + KV_LORA:]
    kr_sw = _swap_halves(kr)
    w_in2 = jnp.concatenate([w_in[:, :Q_LORA + KV_LORA], kr, kr, kr_sw, kr_sw], axis=1)
    uq = w_uq.reshape(Q_LORA, MLA_HEADS, QK_NOPE + QK_ROPE)
    q_nope = uq[:, :, :QK_NOPE].reshape(Q_LORA, -1)
    q_rope = uq[:, :, QK_NOPE:].reshape(Q_LORA, -1)
    w_uq2 = jnp.concatenate([q_nope, q_rope, _swap_halves(q_rope)], axis=1)
    ukv = w_ukv.reshape(KV_LORA, MLA_HEADS, QK_NOPE + V_DIM)
    w_ukv2 = jnp.concatenate([ukv[:, :, :QK_NOPE].reshape(KV_LORA, -1),
                              ukv[:, :, QK_NOPE:].reshape(KV_LORA, -1)], axis=1)
    return w_in2.astype(BF16), w_uq2.astype(BF16), w_ukv2.astype(BF16)


def _pad_lanes(v):
    return jnp.pad(v, (0, LANES - v.shape[0]))[None, :]


def kernel(x, positions, mix_norm, ffn_norm, final_norm, mla_w_in, mla_q_norm, mla_kv_norm, mla_w_uq,
           mla_w_ukv, mla_w_o, ssm_w_in, ssm_conv_w, ssm_conv_b, ssm_dt_bias, ssm_A_log, ssm_D, ssm_norm,
           ssm_w_out, ffn_w_gate, ffn_w_up, ffn_w_down):
    b, s, d = x.shape
    n = b * s
    cos, sin = _rope_tables(positions)
    tril = jnp.tril(jnp.ones((SSM_CHUNK, SSM_CHUNK), BF16))
    expand = jnp.tile(jnp.repeat(jnp.eye(LANES, SSM_HEADS, dtype=BF16), SSM_CHUNK, axis=1), (3, 1))
    shift = _conv_shift_matrix()
    mla_w_o16, ssm_w_in16, ssm_w_out16 = (w.astype(BF16) for w in (mla_w_o, ssm_w_in, ssm_w_out))
    ffn_wg16, ffn_wu16, ffn_wd16 = (w.astype(BF16) for w in (ffn_w_gate, ffn_w_up, ffn_w_down))

    h = x.reshape(n, d)
    for i in range(DEPTH):
        k = i // N_MIXERS
        g_mix = mix_norm[i][None, :]
        if i % N_MIXERS == 0:
            w_in, w_uq, w_ukv = _mla_weights(mla_w_in[k], mla_w_uq[k], mla_w_ukv[k])
            q, kk, v = _mla_proj(h, g_mix, w_in, mla_q_norm[k][None, :], mla_kv_norm[k][None, :],
                                 w_uq, w_ukv, cos, sin)
            a = _attention(q.reshape(b, s, -1), kk.reshape(b, s, -1), v.reshape(b, s, -1))
            w_proj = mla_w_o16
        else:
            wdt = jnp.pad(ssm_w_in[k][:, SSM_INNER + SSM_CONV_DIM:],
                          ((0, 0), (0, LANES - SSM_HEADS))).astype(BF16)
            z, xbc, dt = _ssm_in(h, g_mix, ssm_w_in16, k, wdt)
            a = _ssd(xbc.reshape(b, s, -1), z.reshape(b, s, -1), dt.reshape(b, s, -1), ssm_conv_w[k],
                     ssm_conv_b[k][None, :], shift, _pad_lanes(ssm_dt_bias[k]), _pad_lanes(ssm_A_log[k]),
                     jnp.repeat(ssm_D[k], SSM_HEADDIM)[None, :], ssm_norm[k][None, :], tril, expand)
            w_proj = ssm_w_out16
        h = _proj_ffn(h, a.reshape(n, -1), w_proj, k, ffn_norm[i][None, :], ffn_wg16, ffn_wu16, ffn_wd16, i,
                      final_norm[None, :] if i == DEPTH - 1 else None)
    return h.reshape(b, s, d)
```

```python
import functools
import math

import jax
import jax.numpy as jnp
from jax import lax
from jax.experimental import pallas as pl
from jax.experimental.pallas import tpu as pltpu

F32 = jnp.float32
BF16 = jnp.bfloat16

D_MODEL = 1024
DEPTH = 4
N_MIXERS = 2

MLA_HEADS = 8
Q_LORA = 512
KV_LORA = 256
QK_NOPE = 128
QK_ROPE = 64
V_DIM = 128
ROPE_THETA = 10000.0
ROPE_HALF = QK_ROPE // 2
HEAD_PAIRS = MLA_HEADS // 2
Q_PAIR_W = 2 * QK_NOPE + 2 * QK_ROPE
K_PAIR_W = 2 * QK_NOPE + 4 * QK_ROPE
QK_CAT = QK_NOPE + 2 * QK_ROPE
LOG2_E = math.log2(math.e)
Q_SCALE = (QK_NOPE + QK_ROPE) ** -0.5 * LOG2_E

SSM_INNER = 2 * D_MODEL
SSM_HEADDIM = 64
SSM_HEADS = SSM_INNER // SSM_HEADDIM
SSM_GROUPS = 8
SSM_HPG = SSM_HEADS // SSM_GROUPS
SSM_STATE = 128
SSM_CONV = 4
SSM_CHUNK = 128
SSM_BC_DIM = SSM_GROUPS * SSM_STATE
SSM_CONV_DIM = SSM_INNER + 2 * SSM_BC_DIM
SSM_GROUP_W = SSM_HPG * SSM_HEADDIM
SSM_NORM_EPS = 1e-5

FFN_HIDDEN = 2816
NORM_EPS = 1e-6

LANES = 128
SUBLANES = 8
VMEM_LIMIT_BYTES = 56 * 1024 * 1024

TOKEN_TILE = 512
ATTN_TILE = 512
FFN_CHUNKS = (512, 512, 512, 512, 512, 256)
CONV_HIST = 2 * SUBLANES
CONV_COLS = 512
SSD_STEP_CHUNKS = 4


def _params(*sem, flags=None):
    return pltpu.CompilerParams(dimension_semantics=sem, vmem_limit_bytes=VMEM_LIMIT_BYTES, flags=flags)


def _resident(shape):
    return pl.BlockSpec(shape, lambda *_: (0,) * len(shape), pipeline_mode=pl.Buffered(1))


def _resident_layer(stack, k):
    return pl.BlockSpec((None,) + stack.shape[1:], lambda *_: (k, 0, 0), pipeline_mode=pl.Buffered(1))


def _rms(x, g, eps):
    return x * lax.rsqrt(jnp.mean(x * x, axis=-1, keepdims=True) + eps) * g


def _silu(x):
    return x * jax.nn.sigmoid(x)


def _dot(a, b):
    return jnp.dot(a, b, preferred_element_type=F32)


def _dot_nt(a, b):
    return lax.dot_general(a, b, (((1,), (1,)), ((), ())), preferred_element_type=F32)


def _rope_kernel(pos_ref, inv_ref, sgn_ref, cos_ref, sin_ref):
    ang = pos_ref[...].astype(F32) * inv_ref[...]
    cos_ref[...] = jnp.cos(ang)
    sin_ref[...] = jnp.sin(ang) * sgn_ref[...]


def _rope_tables(positions):
    n = positions.size
    tm = min(n, 2048)
    inv = jnp.power(ROPE_THETA, -jnp.arange(ROPE_HALF, dtype=F32) / ROPE_HALF)
    inv = jnp.tile(inv, LANES // ROPE_HALF)[None, :]
    sgn = jnp.tile(jnp.concatenate([-jnp.ones(ROPE_HALF, F32), jnp.ones(ROPE_HALF, F32)]),
                   LANES // QK_ROPE)[None, :]
    row = pl.BlockSpec((tm, LANES), lambda i: (i, 0))
    return pl.pallas_call(
        _rope_kernel,
        grid=(n // tm,),
        in_specs=[pl.BlockSpec((tm, 1), lambda i: (i, 0)), _resident((1, LANES)), _resident((1, LANES))],
        out_specs=[row, row],
        out_shape=[jax.ShapeDtypeStruct((n, LANES), F32)] * 2,
        compiler_params=_params("parallel"),
        name="rope_tables",
    )(positions.reshape(n, 1), inv, sgn)


def _mla_proj_kernel(x_ref, g_ref, win_ref, qn_ref, kvn_ref, wuq_ref, wukv_ref, cos_ref, sin_ref,
                     q_ref, k_ref, v_ref):
    u = _rms(x_ref[...], g_ref[...], NORM_EPS).astype(BF16)
    lat = _dot(u, win_ref[...])
    cos = cos_ref[...]
    sin = sin_ref[...]
    kv0 = Q_LORA
    kr0 = Q_LORA + KV_LORA
    qn = _rms(lat[:, :Q_LORA], qn_ref[...], NORM_EPS).astype(BF16)
    kvn = _rms(lat[:, kv0:kr0], kvn_ref[...], NORM_EPS).astype(BF16)
    q = _dot(qn, wuq_ref[...])
    kv = _dot(kvn, wukv_ref[...])

    kr = lat[:, kr0:kr0 + LANES] * cos + lat[:, kr0 + LANES:kr0 + 2 * LANES] * sin
    lane = lax.broadcasted_iota(jnp.int32, kr.shape, 1)
    kr_lo = jnp.where(lane < QK_ROPE, kr, 0.0).astype(BF16)
    kr_hi = jnp.where(lane >= QK_ROPE, kr, 0.0).astype(BF16)

    nope_w = MLA_HEADS * QK_NOPE
    rope_w = MLA_HEADS * QK_ROPE
    for p in range(HEAD_PAIRS):
        a = 2 * p * QK_NOPE
        b = a + QK_NOPE
        r = nope_w + p * LANES
        qr = q[:, r:r + LANES] * cos + q[:, r + rope_w:r + rope_w + LANES] * sin
        q0 = p * Q_PAIR_W
        q_ref[:, q0:q0 + LANES] = (q[:, a:a + QK_NOPE] * Q_SCALE).astype(BF16)
        q_ref[:, q0 + LANES:q0 + 2 * LANES] = (qr * Q_SCALE).astype(BF16)
        q_ref[:, q0 + 2 * LANES:q0 + 3 * LANES] = (q[:, b:b + QK_NOPE] * Q_SCALE).astype(BF16)
        k0 = p * K_PAIR_W
        k_ref[:, k0:k0 + LANES] = kv[:, a:a + QK_NOPE].astype(BF16)
        k_ref[:, k0 + LANES:k0 + 2 * LANES] = kr_lo
        k_ref[:, k0 + 2 * LANES:k0 + 3 * LANES] = kr_hi
        k_ref[:, k0 + 3 * LANES:k0 + 4 * LANES] = kv[:, b:b + QK_NOPE].astype(BF16)
    v_ref[...] = kv[:, nope_w:].astype(BF16)


def _mla_proj(h, g, w_in, q_norm, kv_norm, w_uq, w_ukv, cos, sin):
    n = h.shape[0]
    tm = min(n, TOKEN_TILE)
    qw, kw, vw = HEAD_PAIRS * Q_PAIR_W, HEAD_PAIRS * K_PAIR_W, MLA_HEADS * V_DIM
    row = lambda w: pl.BlockSpec((tm, w), lambda i: (i, 0))
    return pl.pallas_call(
        _mla_proj_kernel,
        grid=(n // tm,),
        in_specs=[row(D_MODEL), _resident(g.shape), _resident(w_in.shape), _resident(q_norm.shape),
                  _resident(kv_norm.shape), _resident(w_uq.shape), _resident(w_ukv.shape),
                  row(LANES), row(LANES)],
        out_specs=[row(qw), row(kw), row(vw)],
        out_shape=[jax.ShapeDtypeStruct((n, qw), BF16), jax.ShapeDtypeStruct((n, kw), BF16),
                   jax.ShapeDtypeStruct((n, vw), BF16)],
        compiler_params=_params("parallel"),
        name="mla_proj",
    )(h, g, w_in, q_norm, kv_norm, w_uq, w_ukv, cos, sin)


def _attn_kernel(q_ref, k_ref, v_ref, o_ref):
    t = ATTN_TILE
    rows = lax.broadcasted_iota(jnp.int32, (t, t), 0)
    cols = lax.broadcasted_iota(jnp.int32, (t, t), 1)
    causal = cols <= rows

    for i in range(q_ref.shape[0] // t):
        r0 = i * t
        for hh in range(2):
            qh = q_ref[r0:r0 + t, hh * LANES:hh * LANES + QK_CAT]
            kc = slice(hh * QK_CAT, (hh + 1) * QK_CAT)
            vc = slice(hh * V_DIM, (hh + 1) * V_DIM)
            s = _dot_nt(qh, k_ref[0:r0 + t, kc])
            s_d = jnp.where(causal, s[:, r0:], -jnp.inf)
            m = jnp.max(s_d, axis=-1, keepdims=True)
            if i:
                m = jnp.maximum(m, jnp.max(s[:, :r0], axis=-1, keepdims=True))
            p_d = jnp.exp2(s_d - m)
            l = jnp.sum(p_d, axis=-1, keepdims=True)
            p = p_d.astype(BF16)
            if i:
                p_p = jnp.exp2(s[:, :r0] - m)
                l = l + jnp.sum(p_p, axis=-1, keepdims=True)
                p = jnp.concatenate([p_p.astype(BF16), p], axis=1)
            o_ref[r0:r0 + t, vc] = (_dot(p, v_ref[0:r0 + t, vc]) / l).astype(BF16)


def _attention(q, k, v):
    b, s, _ = q.shape
    blk = lambda w: pl.BlockSpec((None, s, w), lambda bi, p: (bi, 0, p))
    return pl.pallas_call(
        _attn_kernel,
        grid=(b, HEAD_PAIRS),
        in_specs=[blk(Q_PAIR_W), blk(K_PAIR_W), blk(2 * V_DIM)],
        out_specs=blk(2 * V_DIM),
        out_shape=jax.ShapeDtypeStruct((b, s, MLA_HEADS * V_DIM), BF16),
        compiler_params=_params("parallel", "parallel"),
        name="mla_attention",
    )(q, k, v)


def _ffn_kernel(*refs, final_norm):
    h_ref, a_ref, wp_ref, g_ref, wg_ref, wu_ref, wd_ref = refs[:7]
    o_ref, act_ref = refs[-2:]
    x = h_ref[...] + _dot(a_ref[...], wp_ref[...])
    u = _rms(x, g_ref[...], NORM_EPS).astype(BF16)
    off = 0
    for c in FFN_CHUNKS:
        gate = _dot(u, wg_ref[:, off:off + c])
        up = _dot(u, wu_ref[:, off:off + c])
        act_ref[:, off:off + c] = (_silu(gate) * up).astype(BF16)
        off += c
    y = x + _dot(act_ref[...], wd_ref[...])
    if final_norm:
        y = _rms(y, refs[7][...], NORM_EPS)
    o_ref[...] = y


def _proj_ffn(h, a, w_proj, k, g, w_gate, w_up, w_down, i, final_g=None):
    n, kdim = a.shape
    tm = min(n, TOKEN_TILE)
    row = lambda w: pl.BlockSpec((tm, w), lambda t: (t, 0))
    final_norm = final_g is not None
    args = [h, a, w_proj, g, w_gate, w_up, w_down] + ([final_g] if final_norm else [])
    return pl.pallas_call(
        functools.partial(_ffn_kernel, final_norm=final_norm),
        grid=(n // tm,),
        in_specs=[row(D_MODEL), row(kdim), _resident_layer(w_proj, k), _resident(g.shape),
                  _resident_layer(w_gate, i), _resident_layer(w_up, i), _resident_layer(w_down, i)]
        + ([_resident(final_g.shape)] if final_norm else []),
        out_specs=row(D_MODEL),
        out_shape=jax.ShapeDtypeStruct((n, D_MODEL), F32),
        scratch_shapes=[pltpu.VMEM((tm, FFN_HIDDEN), BF16)],
        compiler_params=_params("parallel"),
        name="proj_ffn_final" if final_norm else "proj_ffn",
    )(*args)


def _ssm_in_kernel(x_ref, g_ref, w_ref, wdt_ref, z_ref, xbc_ref, dt_ref):
    u = _rms(x_ref[...], g_ref[...], NORM_EPS).astype(BF16)
    z_ref[...] = _silu(_dot(u, w_ref[:, :SSM_INNER])).astype(BF16)
    step = SSM_CONV_DIM // 4
    for c in range(0, SSM_CONV_DIM, step):
        xbc_ref[:, c:c + step] = _dot(u, w_ref[:, SSM_INNER + c:SSM_INNER + c + step]).astype(BF16)
    dt_ref[...] = _dot(u, wdt_ref[...])


def _ssm_in(h, g, w_in, k, wdt):
    n = h.shape[0]
    tm = min(n, TOKEN_TILE)
    row = lambda w: pl.BlockSpec((tm, w), lambda t: (t, 0))
    return pl.pallas_call(
        _ssm_in_kernel,
        grid=(n // tm,),
        in_specs=[row(D_MODEL), _resident(g.shape), _resident_layer(w_in, k), _resident(wdt.shape)],
        out_specs=[row(SSM_INNER), row(SSM_CONV_DIM), row(LANES)],
        out_shape=[jax.ShapeDtypeStruct((n, SSM_INNER), BF16), jax.ShapeDtypeStruct((n, SSM_CONV_DIM), BF16),
                   jax.ShapeDtypeStruct((n, LANES), F32)],
        compiler_params=_params("parallel"),
        name="ssm_in",
    )(h, g, w_in, wdt)


def _split3(x):
    hi = x.astype(BF16)
    r = x - hi.astype(F32)
    mid = r.astype(BF16)
    lo = (r - mid.astype(F32)).astype(BF16)
    return hi, mid, lo


def _ssd_kernel(xbc_ref, z_ref, dt_ref, cw_ref, cb_ref, shift_ref, dtb_ref, alog_ref, dskip_ref, nw_ref,
                tril_ref, expand_ref, y_ref, state_ref, hist_ref, act_ref):
    L = SSM_CHUNK
    n_chunks = xbc_ref.shape[0] // L

    @pl.when(pl.program_id(1) == 0)
    def _():
        state_ref[...] = jnp.zeros_like(state_ref)
        hist_ref[...] = jnp.zeros_like(hist_ref)

    shift = shift_ref[...]
    cw16 = cw_ref[0:SSM_CONV - 1, :].astype(BF16)
    for cc in range(n_chunks):
        for c0 in range(0, SSM_CONV_DIM, CONV_COLS):
            cs = slice(c0, c0 + CONV_COLS)
            cur = xbc_ref[cc * L:(cc + 1) * L, cs]
            prev = hist_ref[:, cs] if cc == 0 else xbc_ref[cc * L - CONV_HIST:cc * L, cs]
            ext = jnp.concatenate([prev, cur], axis=0)
            taps = [ext * cw16[SSM_CONV - 1 - d:SSM_CONV - d, cs] for d in range(1, SSM_CONV)]
            acc = (_dot(shift, jnp.concatenate(taps, axis=0))
                   + cur.astype(F32) * cw_ref[SSM_CONV - 1:SSM_CONV, cs] + cb_ref[:, cs])
            act_ref[cc * L:(cc + 1) * L, cs] = _silu(acc).astype(BF16)
    hist_ref[...] = xbc_ref[n_chunks * L - CONV_HIST:n_chunks * L, :]

    rows = lax.broadcasted_iota(jnp.int32, (L, L), 0)
    cols = lax.broadcasted_iota(jnp.int32, (L, L), 1)
    causal = rows >= cols
    first = lax.broadcasted_iota(jnp.int32, (L, LANES), 1) < SSM_HEADDIM
    neg_a = -jnp.exp(alog_ref[...])
    tril = tril_ref[...]

    for cc in range(n_chunks):
        r = slice(cc * L, (cc + 1) * L)
        dtv = dt_ref[r, :] + dtb_ref[...]
        dt = jnp.maximum(dtv, 0.0) + jnp.log1p(jnp.exp(-jnp.abs(dtv)))
        hi, mid, lo = _split3(dt * neg_a)
        cum = _dot(tril, hi) + _dot(tril, mid) + _dot(tril, lo)
        cum_t = cum.T
        dt_t = dt.T
        last_t = cum_t[:, L - 1:L]
        ws16 = (jnp.exp(last_t - cum_t) * dt_t).astype(BF16)
        elast_t = jnp.exp(last_t)
        cum2 = cum * LOG2_E
        row2_t = jnp.where(dt_t > 0.0, cum_t * LOG2_E - jnp.log2(dt_t), jnp.inf)
        col2 = _dot(jnp.concatenate(_split3(cum2), axis=1), expand_ref[...])

        for g in range(SSM_GROUPS):
            x0 = g * SSM_GROUP_W
            b0 = SSM_INNER + g * SSM_STATE
            b16 = act_ref[r, b0:b0 + SSM_STATE]
            c16 = act_ref[r, b0 + SSM_BC_DIM:b0 + SSM_BC_DIM + SSM_STATE]
            cb = _dot_nt(c16, b16)
            b_t16 = b16.astype(F32).T.astype(BF16)
            y_in = _dot(c16, state_ref[g].astype(BF16))
            gated = []
            for pr in range(SSM_HPG // 2):
                c0 = x0 + pr * LANES
                sl = slice(pr * LANES, (pr + 1) * LANES)
                xp16 = act_ref[r, c0:c0 + LANES]
                zero = jnp.zeros_like(xp16)
                xbd = jnp.concatenate([jnp.where(first, xp16, zero), jnp.where(first, zero, xp16)], axis=0)
                w, bs, e_col, e_last = [], [], [], []
                for j in range(2):
                    hd = g * SSM_HPG + 2 * pr + j
                    col = col2[:, hd * L:(hd + 1) * L]
                    decay_dt = jnp.exp2(jnp.where(causal, col - row2_t[hd:hd + 1, :], -jnp.inf))
                    w.append((cb * decay_dt).astype(BF16))
                    bs.append(b_t16 * ws16[hd:hd + 1, :])
                    e_col.append(jnp.exp2(col))
                    e_last.append(elast_t[hd:hd + 1, :])
                y = (_dot(jnp.concatenate(w, axis=1), xbd)
                     + y_in[:, sl] * jnp.where(first, e_col[0], e_col[1])
                     + dskip_ref[:, c0:c0 + LANES] * xp16.astype(F32))
                state_ref[g, :, sl] = (state_ref[g, :, sl] * jnp.where(first, e_last[0], e_last[1])
                                       + _dot(jnp.concatenate(bs, axis=1), xbd))
                gated.append(y * z_ref[r, c0:c0 + LANES].astype(F32))
            gated = jnp.concatenate(gated, axis=1)
            ms = jnp.mean(gated * gated, axis=-1, keepdims=True)
            y_ref[r, x0:x0 + SSM_GROUP_W] = (gated * lax.rsqrt(ms + SSM_NORM_EPS)
                                             * nw_ref[:, x0:x0 + SSM_GROUP_W]).astype(y_ref.dtype)


def _ssd(xbc, z, dt, conv_w, conv_b, shift, dt_bias, a_log, d_skip, norm_w, tril, expand):
    b, s, _ = xbc.shape
    rows = SSD_STEP_CHUNKS * SSM_CHUNK
    blk = lambda w: pl.BlockSpec((None, rows, w), lambda bi, c: (bi, c, 0))
    consts = [conv_w, conv_b, shift, dt_bias, a_log, d_skip, norm_w, tril, expand]
    return pl.pallas_call(
        _ssd_kernel,
        grid=(b, s // rows),
        in_specs=[blk(SSM_CONV_DIM), blk(SSM_INNER), blk(LANES)] + [_resident(c.shape) for c in consts],
        out_specs=blk(SSM_INNER),
        out_shape=jax.ShapeDtypeStruct((b, s, SSM_INNER), BF16),
        scratch_shapes=[pltpu.VMEM((SSM_GROUPS, SSM_STATE, SSM_GROUP_W), F32),
                        pltpu.VMEM((CONV_HIST, SSM_CONV_DIM), BF16),
                        pltpu.VMEM((rows, SSM_CONV_DIM), BF16)],
        compiler_params=_params("parallel", "arbitrary"),
        name="ssd",
    )(xbc, z, dt, *consts)


def _conv_shift_matrix():
    e = CONV_HIST + SSM_CHUNK
    t = jnp.arange(SSM_CHUNK)[:, None]
    j = jnp.arange(e)[None, :]
    blocks = [(j == t - d + CONV_HIST) for d in range(1, SSM_CONV)]
    return jnp.concatenate(blocks, axis=1).astype(BF16)


def _swap_halves(w):
    lead = w.shape[:-1]
    w = w.reshape(*lead, -1, 2, ROPE_HALF)
    return jnp.flip(w, axis=-2).reshape(*lead, -1)


def _mla_weights(w_in, w_uq, w_ukv):
    kr = w_in[:, Q_LORA + KV_LORA:]
    kr_sw = _swap_halves(kr)
    w_in2 = jnp.concatenate([w_in[:, :Q_LORA + KV_LORA], kr, kr, kr_sw, kr_sw], axis=1)
    uq = w_uq.reshape(Q_LORA, MLA_HEADS, QK_NOPE + QK_ROPE)
    q_nope = uq[:, :, :QK_NOPE].reshape(Q_LORA, -1)
    q_rope = uq[:, :, QK_NOPE:].reshape(Q_LORA, -1)
    w_uq2 = jnp.concatenate([q_nope, q_rope, _swap_halves(q_rope)], axis=1)
    ukv = w_ukv.reshape(KV_LORA, MLA_HEADS, QK_NOPE + V_DIM)
    w_ukv2 = jnp.concatenate([ukv[:, :, :QK_NOPE].reshape(KV_LORA, -1),
                              ukv[:, :, QK_NOPE:].reshape(KV_LORA, -1)], axis=1)
    return w_in2.astype(BF16), w_uq2.astype(BF16), w_ukv2.astype(BF16)


def _pad_lanes(v):
    return jnp.pad(v, (0, LANES - v.shape[0]))[None, :]


def kernel(x, positions, mix_norm, ffn_norm, final_norm, mla_w_in, mla_q_norm, mla_kv_norm, mla_w_uq,
           mla_w_ukv, mla_w_o, ssm_w_in, ssm_conv_w, ssm_conv_b, ssm_dt_bias, ssm_A_log, ssm_D, ssm_norm,
           ssm_w_out, ffn_w_gate, ffn_w_up, ffn_w_down):
    b, s, d = x.shape
    n = b * s
    cos, sin = _rope_tables(positions)
    tril = jnp.tril(jnp.ones((SSM_CHUNK, SSM_CHUNK), BF16))
    expand = jnp.tile(jnp.repeat(jnp.eye(LANES, SSM_HEADS, dtype=BF16), SSM_CHUNK, axis=1), (3, 1))
    shift = _conv_shift_matrix()
    mla_w_o16, ssm_w_in16, ssm_w_out16 = (w.astype(BF16) for w in (mla_w_o, ssm_w_in, ssm_w_out))
    ffn_wg16, ffn_wu16, ffn_wd16 = (w.astype(BF16) for w in (ffn_w_gate, ffn_w_up, ffn_w_down))

    h = x.reshape(n, d)
    for i in range(DEPTH):
        k = i // N_MIXERS
        g_mix = mix_norm[i][None, :]
        if i % N_MIXERS == 0:
            w_in, w_uq, w_ukv = _mla_weights(mla_w_in[k], mla_w_uq[k], mla_w_ukv[k])
            q, kk, v = _mla_proj(h, g_mix, w_in, mla_q_norm[k][None, :], mla_kv_norm[k][None, :],
                                 w_uq, w_ukv, cos, sin)
            a = _attention(q.reshape(b, s, -1), kk.reshape(b, s, -1), v.reshape(b, s, -1))
            w_proj = mla_w_o16
        else:
            wdt = jnp.pad(ssm_w_in[k][:, SSM_INNER + SSM_CONV_DIM:],
                          ((0, 0), (0, LANES - SSM_HEADS))).astype(BF16)
            z, xbc, dt = _ssm_in(h, g_mix, ssm_w_in16, k, wdt)
            a = _ssd(xbc.reshape(b, s, -1), z.reshape(b, s, -1), dt.reshape(b, s, -1), ssm_conv_w[k],
                     ssm_conv_b[k][None, :], shift, _pad_lanes(ssm_dt_bias[k]), _pad_lanes(ssm_A_log[k]),
                     jnp.repeat(ssm_D[k], SSM_HEADDIM)[None, :], ssm_norm[k][None, :], tril, expand)
            w_proj = ssm_w_out16
        h = _proj_ffn(h, a.reshape(n, -1), w_proj, k, ffn_norm[i][None, :], ffn_wg16, ffn_wu16, ffn_wd16, i,
                      final_norm[None, :] if i == DEPTH - 1 else None)
    return h.reshape(b, s, d)
```

```python
import functools
import math

import jax
import jax.numpy as jnp
from jax import lax
from jax.experimental import pallas as pl
from jax.experimental.pallas import tpu as pltpu

F32 = jnp.float32
BF16 = jnp.bfloat16

D_MODEL = 1024
DEPTH = 4
N_MIXERS = 2

MLA_HEADS = 8
Q_LORA = 512
KV_LORA = 256
QK_NOPE = 128
QK_ROPE = 64
V_DIM = 128
ROPE_THETA = 10000.0
ROPE_HALF = QK_ROPE // 2
HEAD_PAIRS = MLA_HEADS // 2
Q_PAIR_W = 2 * QK_NOPE + 2 * QK_ROPE
K_PAIR_W = 2 * QK_NOPE + 4 * QK_ROPE
QK_CAT = QK_NOPE + 2 * QK_ROPE
LOG2_E = math.log2(math.e)
Q_SCALE = (QK_NOPE + QK_ROPE) ** -0.5 * LOG2_E

SSM_INNER = 2 * D_MODEL
SSM_HEADDIM = 64
SSM_HEADS = SSM_INNER // SSM_HEADDIM
SSM_GROUPS = 8
SSM_HPG = SSM_HEADS // SSM_GROUPS
SSM_STATE = 128
SSM_CONV = 4
SSM_CHUNK = 128
SSM_BC_DIM = SSM_GROUPS * SSM_STATE
SSM_CONV_DIM = SSM_INNER + 2 * SSM_BC_DIM
SSM_GROUP_W = SSM_HPG * SSM_HEADDIM
SSM_NORM_EPS = 1e-5

FFN_HIDDEN = 2816
NORM_EPS = 1e-6

LANES = 128
SUBLANES = 8
VMEM_LIMIT_BYTES = 56 * 1024 * 1024

TOKEN_TILE = 512
ATTN_TILE = 512
FFN_CHUNKS = (512, 512, 512, 512, 512, 256)
CONV_HIST = 2 * SUBLANES
CONV_COLS = 512
SSD_STEP_CHUNKS = 4


def _params(*sem, flags=None):
    return pltpu.CompilerParams(dimension_semantics=sem, vmem_limit_bytes=VMEM_LIMIT_BYTES, flags=flags)


def _resident(shape):
    return pl.BlockSpec(shape, lambda *_: (0,) * len(shape), pipeline_mode=pl.Buffered(1))


def _resident_layer(stack, k):
    return pl.BlockSpec((None,) + stack.shape[1:], lambda *_: (k, 0, 0), pipeline_mode=pl.Buffered(1))


def _rms(x, g, eps):
    return x * lax.rsqrt(jnp.mean(x * x, axis=-1, keepdims=True) + eps) * g


def _silu(x):
    return x * jax.nn.sigmoid(x)


def _dot(a, b):
    return jnp.dot(a, b, preferred_element_type=F32)


def _dot_nt(a, b):
    return lax.dot_general(a, b, (((1,), (1,)), ((), ())), preferred_element_type=F32)


def _rope_kernel(pos_ref, inv_ref, sgn_ref, cos_ref, sin_ref):
    ang = pos_ref[...].astype(F32) * inv_ref[...]
    cos_ref[...] = jnp.cos(ang)
    sin_ref[...] = jnp.sin(ang) * sgn_ref[...]


def _rope_tables(positions):
    n = positions.size
    tm = min(n, 2048)
    inv = jnp.power(ROPE_THETA, -jnp.arange(ROPE_HALF, dtype=F32) / ROPE_HALF)
    inv = jnp.tile(inv, LANES // ROPE_HALF)[None, :]
    sgn = jnp.tile(jnp.concatenate([-jnp.ones(ROPE_HALF, F32), jnp.ones(ROPE_HALF, F32)]),
                   LANES // QK_ROPE)[None, :]
    row = pl.BlockSpec((tm, LANES), lambda i: (i, 0))
    return pl.pallas_call(
        _rope_kernel,
        grid=(n // tm,),
        in_specs=[pl.BlockSpec((tm, 1), lambda i: (i, 0)), _resident((1, LANES)), _resident((1, LANES))],
        out_specs=[row, row],
        out_shape=[jax.ShapeDtypeStruct((n, LANES), F32)] * 2,
        compiler_params=_params("parallel"),
        name="rope_tables",
    )(positions.reshape(n, 1), inv, sgn)


def _mla_proj_kernel(x_ref, g_ref, win_ref, qn_ref, kvn_ref, wuq_ref, wukv_ref, cos_ref, sin_ref,
                     q_ref, k_ref, v_ref):
    u = _rms(x_ref[...], g_ref[...], NORM_EPS).astype(BF16)
    lat = _dot(u, win_ref[...])
    cos = cos_ref[...]
    sin = sin_ref[...]
    kv0 = Q_LORA
    kr0 = Q_LORA + KV_LORA
    qn = _rms(lat[:, :Q_LORA], qn_ref[...], NORM_EPS).astype(BF16)
    kvn = _rms(lat[:, kv0:kr0], kvn_ref[...], NORM_EPS).astype(BF16)
    q = _dot(qn, wuq_ref[...])
    kv = _dot(kvn, wukv_ref[...])

    kr = lat[:, kr0:kr0 + LANES] * cos + lat[:, kr0 + LANES:kr0 + 2 * LANES] * sin
    lane = lax.broadcasted_iota(jnp.int32, kr.shape, 1)
    kr_lo = jnp.where(lane < QK_ROPE, kr, 0.0).astype(BF16)
    kr_hi = jnp.where(lane >= QK_ROPE, kr, 0.0).astype(BF16)

    nope_w = MLA_HEADS * QK_NOPE
    rope_w = MLA_HEADS * QK_ROPE
    for p in range(HEAD_PAIRS):
        a = 2 * p * QK_NOPE
        b = a + QK_NOPE
        r = nope_w + p * LANES
        qr = q[:, r:r + LANES] * cos + q[:, r + rope_w:r + rope_w + LANES] * sin
        q0 = p * Q_PAIR_W
        q_ref[:, q0:q0 + LANES] = (q[:, a:a + QK_NOPE] * Q_SCALE).astype(BF16)
        q_ref[:, q0 + LANES:q0 + 2 * LANES] = (qr * Q_SCALE).astype(BF16)
        q_ref[:, q0 + 2 * LANES:q0 + 3 * LANES] = (q[:, b:b + QK_NOPE] * Q_SCALE).astype(BF16)
        k0 = p * K_PAIR_W
        k_ref[:, k0:k0 + LANES] = kv[:, a:a + QK_NOPE].astype(BF16)
        k_ref[:, k0 + LANES:k0 + 2 * LANES] = kr_lo
        k_ref[:, k0 + 2 * LANES:k0 + 3 * LANES] = kr_hi
        k_ref[:, k0 + 3 * LANES:k0 + 4 * LANES] = kv[:, b:b + QK_NOPE].astype(BF16)
    v_ref[...] = kv[:, nope_w:].astype(BF16)


def _mla_proj(h, g, w_in, q_norm, kv_norm, w_uq, w_ukv, cos, sin):
    n = h.shape[0]
    tm = min(n, TOKEN_TILE)
    qw, kw, vw = HEAD_PAIRS * Q_PAIR_W, HEAD_PAIRS * K_PAIR_W, MLA_HEADS * V_DIM
    row = lambda w: pl.BlockSpec((tm, w), lambda i: (i, 0))
    return pl.pallas_call(
        _mla_proj_kernel,
        grid=(n // tm,),
        in_specs=[row(D_MODEL), _resident(g.shape), _resident(w_in.shape), _resident(q_norm.shape),
                  _resident(kv_norm.shape), _resident(w_uq.shape), _resident(w_ukv.shape),
                  row(LANES), row(LANES)],
        out_specs=[row(qw), row(kw), row(vw)],
        out_shape=[jax.ShapeDtypeStruct((n, qw), BF16), jax.ShapeDtypeStruct((n, kw), BF16),
                   jax.ShapeDtypeStruct((n, vw), BF16)],
        compiler_params=_params("parallel"),
        name="mla_proj",
    )(h, g, w_in, q_norm, kv_norm, w_uq, w_ukv, cos, sin)


def _attn_kernel(q_ref, k_ref, v_ref, o_ref):
    t = ATTN_TILE
    rows = lax.broadcasted_iota(jnp.int32, (t, t), 0)
    cols = lax.broadcasted_iota(jnp.int32, (t, t), 1)
    causal = cols <= rows

    units = [(i * t, hh) for i in range(q_ref.shape[0] // t) for hh in range(2)]
    kcol = lambda hh: slice(hh * QK_CAT, (hh + 1) * QK_CAT)
    vcol = lambda hh: slice(hh * V_DIM, (hh + 1) * V_DIM)

    scores = [_dot_nt(q_ref[r0:r0 + t, hh * LANES:hh * LANES + QK_CAT], k_ref[0:r0 + t, kcol(hh)])
              for r0, hh in units]
    probs, sums = [], []
    for (r0, hh), s in zip(units, scores):
        s_d = jnp.where(causal, s[:, r0:], -jnp.inf)
        m = jnp.max(s_d, axis=-1, keepdims=True)
        if r0:
            m = jnp.maximum(m, jnp.max(s[:, :r0], axis=-1, keepdims=True))
        p_d = jnp.exp2(s_d - m)
        l = jnp.sum(p_d, axis=-1, keepdims=True)
        p = p_d.astype(BF16)
        if r0:
            p_p = jnp.exp2(s[:, :r0] - m)
            l = l + jnp.sum(p_p, axis=-1, keepdims=True)
            p = jnp.concatenate([p_p.astype(BF16), p], axis=1)
        probs.append(p)
        sums.append(l)
    for (r0, hh), p, l in zip(units, probs, sums):
        o_ref[r0:r0 + t, vcol(hh)] = (_dot(p, v_ref[0:r0 + t, vcol(hh)]) / l).astype(BF16)


def _attention(q, k, v):
    b, s, _ = q.shape
    blk = lambda w: pl.BlockSpec((None, s, w), lambda bi, p: (bi, 0, p))
    return pl.pallas_call(
        _attn_kernel,
        grid=(b, HEAD_PAIRS),
        in_specs=[blk(Q_PAIR_W), blk(K_PAIR_W), blk(2 * V_DIM)],
        out_specs=blk(2 * V_DIM),
        out_shape=jax.ShapeDtypeStruct((b, s, MLA_HEADS * V_DIM), BF16),
        compiler_params=_params("parallel", "parallel"),
        name="mla_attention",
    )(q, k, v)


def _ffn_kernel(*refs, final_norm):
    h_ref, a_ref, wp_ref, g_ref, wg_ref, wu_ref, wd_ref = refs[:7]
    o_ref, act_ref = refs[-2:]
    x = h_ref[...] + _dot(a_ref[...], wp_ref[...])
    u = _rms(x, g_ref[...], NORM_EPS).astype(BF16)
    off = 0
    for c in FFN_CHUNKS:
        gate = _dot(u, wg_ref[:, off:off + c])
        up = _dot(u, wu_ref[:, off:off + c])
        act_ref[:, off:off + c] = (_silu(gate) * up).astype(BF16)
        off += c
    y = x + _dot(act_ref[...], wd_ref[...])
    if final_norm:
        y = _rms(y, refs[7][...], NORM_EPS)
    o_ref[...] = y


def _proj_ffn(h, a, w_proj, k, g, w_gate, w_up, w_down, i, final_g=None):
    n, kdim = a.shape
    tm = min(n, TOKEN_TILE)
    row = lambda w: pl.BlockSpec((tm, w), lambda t: (t, 0))
    final_norm = final_g is not None
    args = [h, a, w_proj, g, w_gate, w_up, w_down] + ([final_g] if final_norm else [])
    return pl.pallas_call(
        functools.partial(_ffn_kernel, final_norm=final_norm),
        grid=(n // tm,),
        in_specs=[row(D_MODEL), row(kdim), _resident_layer(w_proj, k), _resident(g.shape),
                  _resident_layer(w_gate, i), _resident_layer(w_up, i), _resident_layer(w_down, i)]
        + ([_resident(final_g.shape)] if final_norm else []),
        out_specs=row(D_MODEL),
        out_shape=jax.ShapeDtypeStruct((n, D_MODEL), F32),
        scratch_shapes=[pltpu.VMEM((tm, FFN_HIDDEN), BF16)],
        compiler_params=_params("parallel"),
        name="proj_ffn_final" if final_norm else "proj_ffn",
    )(*args)


def _ssm_in_kernel(x_ref, g_ref, w_ref, wdt_ref, z_ref, xbc_ref, dt_ref):
    u = _rms(x_ref[...], g_ref[...], NORM_EPS).astype(BF16)
    z_ref[...] = _silu(_dot(u, w_ref[:, :SSM_INNER])).astype(BF16)
    step = SSM_CONV_DIM // 4
    for c in range(0, SSM_CONV_DIM, step):
        xbc_ref[:, c:c + step] = _dot(u, w_ref[:, SSM_INNER + c:SSM_INNER + c + step]).astype(BF16)
    dt_ref[...] = _dot(u, wdt_ref[...])


def _ssm_in(h, g, w_in, k, wdt):
    n = h.shape[0]
    tm = min(n, TOKEN_TILE)
    row = lambda w: pl.BlockSpec((tm, w), lambda t: (t, 0))
    return pl.pallas_call(
        _ssm_in_kernel,
        grid=(n // tm,),
        in_specs=[row(D_MODEL), _resident(g.shape), _resident_layer(w_in, k), _resident(wdt.shape)],
        out_specs=[row(SSM_INNER), row(SSM_CONV_DIM), row(LANES)],
        out_shape=[jax.ShapeDtypeStruct((n, SSM_INNER), BF16), jax.ShapeDtypeStruct((n, SSM_CONV_DIM), BF16),
                   jax.ShapeDtypeStruct((n, LANES), F32)],
        compiler_params=_params("parallel"),
        name="ssm_in",
    )(h, g, w_in, wdt)


def _split3(x):
    hi = x.astype(BF16)
    r = x - hi.astype(F32)
    mid = r.astype(BF16)
    lo = (r - mid.astype(F32)).astype(BF16)
    return hi, mid, lo


def _ssd_kernel(xbc_ref, z_ref, dt_ref, cw_ref, cb_ref, shift_ref, dtb_ref, alog_ref, dskip_ref, nw_ref,
                tril_ref, expand_ref, y_ref, state_ref, hist_ref, act_ref):
    L = SSM_CHUNK
    n_chunks = xbc_ref.shape[0] // L

    @pl.when(pl.program_id(1) == 0)
    def _():
        state_ref[...] = jnp.zeros_like(state_ref)
        hist_ref[...] = jnp.zeros_like(hist_ref)

    shift = shift_ref[...]
    cw16 = cw_ref[0:SSM_CONV - 1, :].astype(BF16)
    for cc in range(n_chunks):
        for c0 in range(0, SSM_CONV_DIM, CONV_COLS):
            cs = slice(c0, c0 + CONV_COLS)
            cur = xbc_ref[cc * L:(cc + 1) * L, cs]
            prev = hist_ref[:, cs] if cc == 0 else xbc_ref[cc * L - CONV_HIST:cc * L, cs]
            ext = jnp.concatenate([prev, cur], axis=0)
            taps = [ext * cw16[SSM_CONV - 1 - d:SSM_CONV - d, cs] for d in range(1, SSM_CONV)]
            acc = (_dot(shift, jnp.concatenate(taps, axis=0))
                   + cur.astype(F32) * cw_ref[SSM_CONV - 1:SSM_CONV, cs] + cb_ref[:, cs])
            act_ref[cc * L:(cc + 1) * L, cs] = _silu(acc).astype(BF16)
    hist_ref[...] = xbc_ref[n_chunks * L - CONV_HIST:n_chunks * L, :]

    rows = lax.broadcasted_iota(jnp.int32, (L, L), 0)
    cols = lax.broadcasted_iota(jnp.int32, (L, L), 1)
    causal = rows >= cols
    first = lax.broadcasted_iota(jnp.int32, (L, LANES), 1) < SSM_HEADDIM
    neg_a = -jnp.exp(alog_ref[...])
    tril = tril_ref[...]

    for cc in range(n_chunks):
        r = slice(cc * L, (cc + 1) * L)
        dtv = dt_ref[r, :] + dtb_ref[...]
        dt = jnp.maximum(dtv, 0.0) + jnp.log1p(jnp.exp(-jnp.abs(dtv)))
        hi, mid, lo = _split3(dt * neg_a)
        cum = _dot(tril, hi) + _dot(tril, mid) + _dot(tril, lo)
        cum_t = cum.T
        dt_t = dt.T
        last_t = cum_t[:, L - 1:L]
        ws16 = (jnp.exp(last_t - cum_t) * dt_t).astype(BF16)
        elast_t = jnp.exp(last_t)
        cum2 = cum * LOG2_E
        row2_t = jnp.where(dt_t > 0.0, cum_t * LOG2_E - jnp.log2(dt_t), jnp.inf)
        col2 = _dot(jnp.concatenate(_split3(cum2), axis=1), expand_ref[...])

        cbs, b_ts, y_ins = [], [], []
        for g in range(SSM_GROUPS):
            b0 = SSM_INNER + g * SSM_STATE
            b16 = act_ref[r, b0:b0 + SSM_STATE]
            c16 = act_ref[r, b0 + SSM_BC_DIM:b0 + SSM_BC_DIM + SSM_STATE]
            cbs.append(_dot_nt(c16, b16))
            b_ts.append(b16.astype(F32).T.astype(BF16))
            y_ins.append(_dot(c16, state_ref[g].astype(BF16)))

        ws, bss = [], []
        for hd in range(SSM_HEADS):
            g = hd // SSM_HPG
            col = col2[:, hd * L:(hd + 1) * L]
            decay_dt = jnp.exp2(jnp.where(causal, col - row2_t[hd:hd + 1, :], -jnp.inf))
            ws.append((cbs[g] * decay_dt).astype(BF16))
            bss.append(b_ts[g] * ws16[hd:hd + 1, :])

        y_intra, s_new = [], []
        for pp in range(SSM_HEADS // 2):
            ha, hb = 2 * pp, 2 * pp + 1
            xp16 = act_ref[r, pp * LANES:(pp + 1) * LANES]
            zero = jnp.zeros_like(xp16)
            xbd = jnp.concatenate([jnp.where(first, xp16, zero), jnp.where(first, zero, xp16)], axis=0)
            y_intra.append(_dot(jnp.concatenate([ws[ha], ws[hb]], axis=1), xbd))
            s_new.append(_dot(jnp.concatenate([bss[ha], bss[hb]], axis=1), xbd))
        gated = []
        for pp in range(SSM_HEADS // 2):
            g, pr = divmod(pp, SSM_HPG // 2)
            c0 = pp * LANES
            sl = slice(pr * LANES, (pr + 1) * LANES)
            ha, hb = 2 * pp, 2 * pp + 1
            e_col = jnp.where(first, jnp.exp2(col2[:, ha * L:(ha + 1) * L]), jnp.exp2(col2[:, hb * L:(hb + 1) * L]))
            y = (y_intra[pp] + y_ins[g][:, sl] * e_col
                 + dskip_ref[:, c0:c0 + LANES] * act_ref[r, c0:c0 + LANES].astype(F32))
            state_ref[g, :, sl] = (state_ref[g, :, sl] * jnp.where(first, elast_t[ha:ha + 1, :], elast_t[hb:hb + 1, :])
                                   + s_new[pp])
            gated.append(y * z_ref[r, c0:c0 + LANES].astype(F32))

        for g in range(SSM_GROUPS):
            x0 = g * SSM_GROUP_W
            gg = jnp.concatenate(gated[g * (SSM_HPG // 2):(g + 1) * (SSM_HPG // 2)], axis=1)
            ms = jnp.mean(gg * gg, axis=-1, keepdims=True)
            y_ref[r, x0:x0 + SSM_GROUP_W] = (gg * lax.rsqrt(ms + SSM_NORM_EPS)
                                             * nw_ref[:, x0:x0 + SSM_GROUP_W]).astype(y_ref.dtype)


def _ssd(xbc, z, dt, conv_w, conv_b, shift, dt_bias, a_log, d_skip, norm_w, tril, expand):
    b, s, _ = xbc.shape
    rows = SSD_STEP_CHUNKS * SSM_CHUNK
    blk = lambda w: pl.BlockSpec((None, rows, w), lambda bi, c: (bi, c, 0))
    consts = [conv_w, conv_b, shift, dt_bias, a_log, d_skip, norm_w, tril, expand]
    return pl.pallas_call(
        _ssd_kernel,
        grid=(b, s // rows),
        in_specs=[blk(SSM_CONV_DIM), blk(SSM_INNER), blk(LANES)] + [_resident(c.shape) for c in consts],
        out_specs=blk(SSM_INNER),
        out_shape=jax.ShapeDtypeStruct((b, s, SSM_INNER), BF16),
        scratch_shapes=[pltpu.VMEM((SSM_GROUPS, SSM_STATE, SSM_GROUP_W), F32),
                        pltpu.VMEM((CONV_HIST, SSM_CONV_DIM), BF16),
                        pltpu.VMEM((rows, SSM_CONV_DIM), BF16)],
        compiler_params=_params("parallel", "arbitrary"),
        name="ssd",
    )(xbc, z, dt, *consts)


def _conv_shift_matrix():
    e = CONV_HIST + SSM_CHUNK
    t = jnp.arange(SSM_CHUNK)[:, None]
    j = jnp.arange(e)[None, :]
    blocks = [(j == t - d + CONV_HIST) for d in range(1, SSM_CONV)]
    return jnp.concatenate(blocks, axis=1).astype(BF16)


def _swap_halves(w):
    lead = w.shape[:-1]
    w = w.reshape(*lead, -1, 2, ROPE_HALF)
    return jnp.flip(w, axis=-2).reshape(*lead, -1)


def _mla_weights(w_in, w_uq, w_ukv):
    kr = w_in[:, Q_LORA + KV_LORA:]
    kr_sw = _swap_halves(kr)
    w_in2 = jnp.concatenate([w_in[:, :Q_LORA + KV_LORA], kr, kr, kr_sw, kr_sw], axis=1)
    uq = w_uq.reshape(Q_LORA, MLA_HEADS, QK_NOPE + QK_ROPE)
    q_nope = uq[:, :, :QK_NOPE].reshape(Q_LORA, -1)
    q_rope = uq[:, :, QK_NOPE:].reshape(Q_LORA, -1)
    w_uq2 = jnp.concatenate([q_nope, q_rope, _swap_halves(q_rope)], axis=1)
    ukv = w_ukv.reshape(KV_LORA, MLA_HEADS, QK_NOPE + V_DIM)
    w_ukv2 = jnp.concatenate([ukv[:, :, :QK_NOPE].reshape(KV_LORA, -1),
                              ukv[:, :, QK_NOPE:].reshape(KV_LORA, -1)], axis=1)
    return w_in2.astype(BF16), w_uq2.astype(BF16), w_ukv2.astype(BF16)


def _pad_lanes(v):
    return jnp.pad(v, (0, LANES - v.shape[0]))[None, :]


def kernel(x, positions, mix_norm, ffn_norm, final_norm, mla_w_in, mla_q_norm, mla_kv_norm, mla_w_uq,
           mla_w_ukv, mla_w_o, ssm_w_in, ssm_conv_w, ssm_conv_b, ssm_dt_bias, ssm_A_log, ssm_D, ssm_norm,
           ssm_w_out, ffn_w_gate, ffn_w_up, ffn_w_down):
    b, s, d = x.shape
    n = b * s
    cos, sin = _rope_tables(positions)
    tril = jnp.tril(jnp.ones((SSM_CHUNK, SSM_CHUNK), BF16))
    expand = jnp.tile(jnp.repeat(jnp.eye(LANES, SSM_HEADS, dtype=BF16), SSM_CHUNK, axis=1), (3, 1))
    shift = _conv_shift_matrix()
    mla_w_o16, ssm_w_in16, ssm_w_out16 = (w.astype(BF16) for w in (mla_w_o, ssm_w_in, ssm_w_out))
    ffn_wg16, ffn_wu16, ffn_wd16 = (w.astype(BF16) for w in (ffn_w_gate, ffn_w_up, ffn_w_down))

    h = x.reshape(n, d)
    for i in range(DEPTH):
        k = i // N_MIXERS
        g_mix = mix_norm[i][None, :]
        if i % N_MIXERS == 0:
            w_in, w_uq, w_ukv = _mla_weights(mla_w_in[k], mla_w_uq[k], mla_w_ukv[k])
            q, kk, v = _mla_proj(h, g_mix, w_in, mla_q_norm[k][None, :], mla_kv_norm[k][None, :],
                                 w_uq, w_ukv, cos, sin)
            a = _attention(q.reshape(b, s, -1), kk.reshape(b, s, -1), v.reshape(b, s, -1))
            w_proj = mla_w_o16
        else:
            wdt = jnp.pad(ssm_w_in[k][:, SSM_INNER + SSM_CONV_DIM:],
                          ((0, 0), (0, LANES - SSM_HEADS))).astype(BF16)
            z, xbc, dt = _ssm_in(h, g_mix, ssm_w_in16, k, wdt)
            a = _ssd(xbc.reshape(b, s, -1), z.reshape(b, s, -1), dt.reshape(b, s, -1), ssm_conv_w[k],
                     ssm_conv_b[k][None, :], shift, _pad_lanes(ssm_dt_bias[k]), _pad_lanes(ssm_A_log[k]),
                     jnp.repeat(ssm_D[k], SSM_HEADDIM)[None, :], ssm_norm[k][None, :], tril, expand)
            w_proj = ssm_w_out16
        h = _proj_ffn(h, a.reshape(n, -1), w_proj, k, ffn_norm[i][None, :], ffn_wg16, ffn_wu16, ffn_wd16, i,
                      final_norm[None, :] if i == DEPTH - 1 else None)
    return h.reshape(b, s, d)
```

```python
import functools
import math

import jax
import jax.numpy as jnp
from jax import lax
from jax.experimental import pallas as pl
from jax.experimental.pallas import tpu as pltpu

F32 = jnp.float32
BF16 = jnp.bfloat16

D_MODEL = 1024
DEPTH = 4
N_MIXERS = 2

MLA_HEADS = 8
Q_LORA = 512
KV_LORA = 256
QK_NOPE = 128
QK_ROPE = 64
V_DIM = 128
ROPE_THETA = 10000.0
ROPE_HALF = QK_ROPE // 2
HEAD_PAIRS = MLA_HEADS // 2
Q_PAIR_W = 2 * QK_NOPE + 2 * QK_ROPE
K_PAIR_W = 2 * QK_NOPE + 4 * QK_ROPE
QK_CAT = QK_NOPE + 2 * QK_ROPE
LOG2_E = math.log2(math.e)
Q_SCALE = (QK_NOPE + QK_ROPE) ** -0.5 * LOG2_E

SSM_INNER = 2 * D_MODEL
SSM_HEADDIM = 64
SSM_HEADS = SSM_INNER // SSM_HEADDIM
SSM_GROUPS = 8
SSM_HPG = SSM_HEADS // SSM_GROUPS
SSM_STATE = 128
SSM_CONV = 4
SSM_CHUNK = 128
SSM_BC_DIM = SSM_GROUPS * SSM_STATE
SSM_CONV_DIM = SSM_INNER + 2 * SSM_BC_DIM
SSM_GROUP_W = SSM_HPG * SSM_HEADDIM
SSM_NORM_EPS = 1e-5

FFN_HIDDEN = 2816
NORM_EPS = 1e-6

LANES = 128
SUBLANES = 8
VMEM_LIMIT_BYTES = 56 * 1024 * 1024

TOKEN_TILE = 512
ATTN_TILE = 512
FFN_CHUNKS = (512, 512, 512, 512, 512, 256)
CONV_HIST = 2 * SUBLANES
CONV_COLS = 512
SSD_STEP_CHUNKS = 4
assert SSM_CONV_DIM // CONV_COLS == SSM_GROUPS


def _params(*sem, flags=None):
    return pltpu.CompilerParams(dimension_semantics=sem, vmem_limit_bytes=VMEM_LIMIT_BYTES, flags=flags)


def _resident(shape):
    return pl.BlockSpec(shape, lambda *_: (0,) * len(shape), pipeline_mode=pl.Buffered(1))


def _resident_layer(stack, k):
    return pl.BlockSpec((None,) + stack.shape[1:], lambda *_: (k, 0, 0), pipeline_mode=pl.Buffered(1))


def _rms(x, g, eps):
    return x * lax.rsqrt(jnp.mean(x * x, axis=-1, keepdims=True) + eps) * g


def _silu(x):
    return x * jax.nn.sigmoid(x)


def _dot(a, b):
    return jnp.dot(a, b, preferred_element_type=F32)


def _dot_nt(a, b):
    return lax.dot_general(a, b, (((1,), (1,)), ((), ())), preferred_element_type=F32)


def _rope_kernel(pos_ref, inv_ref, sgn_ref, cos_ref, sin_ref):
    ang = pos_ref[...].astype(F32) * inv_ref[...]
    cos_ref[...] = jnp.cos(ang)
    sin_ref[...] = jnp.sin(ang) * sgn_ref[...]


def _rope_tables(positions):
    n = positions.size
    tm = min(n, 2048)
    inv = jnp.power(ROPE_THETA, -jnp.arange(ROPE_HALF, dtype=F32) / ROPE_HALF)
    inv = jnp.tile(inv, LANES // ROPE_HALF)[None, :]
    sgn = jnp.tile(jnp.concatenate([-jnp.ones(ROPE_HALF, F32), jnp.ones(ROPE_HALF, F32)]),
                   LANES // QK_ROPE)[None, :]
    row = pl.BlockSpec((tm, LANES), lambda i: (i, 0))
    return pl.pallas_call(
        _rope_kernel,
        grid=(n // tm,),
        in_specs=[pl.BlockSpec((tm, 1), lambda i: (i, 0)), _resident((1, LANES)), _resident((1, LANES))],
        out_specs=[row, row],
        out_shape=[jax.ShapeDtypeStruct((n, LANES), F32)] * 2,
        compiler_params=_params("parallel"),
        name="rope_tables",
    )(positions.reshape(n, 1), inv, sgn)


def _mla_proj_kernel(x_ref, g_ref, win_ref, qn_ref, kvn_ref, wuq_ref, wukv_ref, cos_ref, sin_ref,
                     q_ref, k_ref, v_ref):
    u = _rms(x_ref[...], g_ref[...], NORM_EPS).astype(BF16)
    lat = _dot(u, win_ref[...])
    cos = cos_ref[...]
    sin = sin_ref[...]
    kv0 = Q_LORA
    kr0 = Q_LORA + KV_LORA
    qn = _rms(lat[:, :Q_LORA], qn_ref[...], NORM_EPS).astype(BF16)
    kvn = _rms(lat[:, kv0:kr0], kvn_ref[...], NORM_EPS).astype(BF16)
    q = _dot(qn, wuq_ref[...])
    kv = _dot(kvn, wukv_ref[...])

    lane = lax.broadcasted_iota(jnp.int32, cos.shape, 1)
    low_half = lane % QK_ROPE < ROPE_HALF

    def rope(x):
        swapped = jnp.where(low_half, pltpu.roll(x, LANES - ROPE_HALF, 1), pltpu.roll(x, ROPE_HALF, 1))
        return x * cos + swapped * sin

    kr = rope(lat[:, kr0:kr0 + LANES])
    kr_lo = jnp.where(lane < QK_ROPE, kr, 0.0).astype(BF16)
    kr_hi = jnp.where(lane >= QK_ROPE, kr, 0.0).astype(BF16)

    nope_w = MLA_HEADS * QK_NOPE
    for p in range(HEAD_PAIRS):
        a = 2 * p * QK_NOPE
        b = a + QK_NOPE
        r = nope_w + p * LANES
        qr = rope(q[:, r:r + LANES])
        q0 = p * Q_PAIR_W
        q_ref[:, q0:q0 + LANES] = (q[:, a:a + QK_NOPE] * Q_SCALE).astype(BF16)
        q_ref[:, q0 + LANES:q0 + 2 * LANES] = (qr * Q_SCALE).astype(BF16)
        q_ref[:, q0 + 2 * LANES:q0 + 3 * LANES] = (q[:, b:b + QK_NOPE] * Q_SCALE).astype(BF16)
        k0 = p * K_PAIR_W
        k_ref[:, k0:k0 + LANES] = kv[:, a:a + QK_NOPE].astype(BF16)
        k_ref[:, k0 + LANES:k0 + 2 * LANES] = kr_lo
        k_ref[:, k0 + 2 * LANES:k0 + 3 * LANES] = kr_hi
        k_ref[:, k0 + 3 * LANES:k0 + 4 * LANES] = kv[:, b:b + QK_NOPE].astype(BF16)
    v_ref[...] = kv[:, nope_w:].astype(BF16)


def _mla_proj(h, g, w_in, q_norm, kv_norm, w_uq, w_ukv, cos, sin):
    n = h.shape[0]
    tm = min(n, TOKEN_TILE)
    qw, kw, vw = HEAD_PAIRS * Q_PAIR_W, HEAD_PAIRS * K_PAIR_W, MLA_HEADS * V_DIM
    row = lambda w: pl.BlockSpec((tm, w), lambda i: (i, 0))
    return pl.pallas_call(
        _mla_proj_kernel,
        grid=(n // tm,),
        in_specs=[row(D_MODEL), _resident(g.shape), _resident(w_in.shape), _resident(q_norm.shape),
                  _resident(kv_norm.shape), _resident(w_uq.shape), _resident(w_ukv.shape),
                  row(LANES), row(LANES)],
        out_specs=[row(qw), row(kw), row(vw)],
        out_shape=[jax.ShapeDtypeStruct((n, qw), BF16), jax.ShapeDtypeStruct((n, kw), BF16),
                   jax.ShapeDtypeStruct((n, vw), BF16)],
        compiler_params=_params("parallel"),
        name="mla_proj",
    )(h, g, w_in, q_norm, kv_norm, w_uq, w_ukv, cos, sin)


def _attn_kernel(q_ref, k_ref, v_ref, o_ref):
    t = ATTN_TILE
    rows = lax.broadcasted_iota(jnp.int32, (t, t), 0)
    cols = lax.broadcasted_iota(jnp.int32, (t, t), 1)
    causal = cols <= rows

    units = [(i * t, hh) for i in range(q_ref.shape[0] // t) for hh in range(2)]
    kcol = lambda hh: slice(hh * QK_CAT, (hh + 1) * QK_CAT)
    vcol = lambda hh: slice(hh * V_DIM, (hh + 1) * V_DIM)

    scores = [_dot_nt(q_ref[r0:r0 + t, hh * LANES:hh * LANES + QK_CAT], k_ref[0:r0 + t, kcol(hh)])
              for r0, hh in units]
    probs, sums = [], []
    for (r0, hh), s in zip(units, scores):
        s_d = jnp.where(causal, s[:, r0:], -jnp.inf)
        m = jnp.max(s_d, axis=-1, keepdims=True)
        if r0:
            m = jnp.maximum(m, jnp.max(s[:, :r0], axis=-1, keepdims=True))
        p_d = jnp.exp2(s_d - m)
        l = jnp.sum(p_d, axis=-1, keepdims=True)
        p = p_d.astype(BF16)
        if r0:
            p_p = jnp.exp2(s[:, :r0] - m)
            l = l + jnp.sum(p_p, axis=-1, keepdims=True)
            p = jnp.concatenate([p_p.astype(BF16), p], axis=1)
        probs.append(p)
        sums.append(l)
    for (r0, hh), p, l in zip(units, probs, sums):
        o_ref[r0:r0 + t, vcol(hh)] = (_dot(p, v_ref[0:r0 + t, vcol(hh)]) / l).astype(BF16)


def _attention(q, k, v):
    b, s, _ = q.shape
    blk = lambda w: pl.BlockSpec((None, s, w), lambda bi, p: (bi, 0, p))
    return pl.pallas_call(
        _attn_kernel,
        grid=(b, HEAD_PAIRS),
        in_specs=[blk(Q_PAIR_W), blk(K_PAIR_W), blk(2 * V_DIM)],
        out_specs=blk(2 * V_DIM),
        out_shape=jax.ShapeDtypeStruct((b, s, MLA_HEADS * V_DIM), BF16),
        compiler_params=_params("parallel", "parallel"),
        name="mla_attention",
    )(q, k, v)


def _ffn_kernel(*refs, final_norm):
    h_ref, a_ref, wp_ref, g_ref, wg_ref, wu_ref, wd_ref = refs[:7]
    o_ref, act_ref = refs[-2:]
    x = h_ref[...] + _dot(a_ref[...], wp_ref[...])
    u = _rms(x, g_ref[...], NORM_EPS).astype(BF16)
    off = 0
    for c in FFN_CHUNKS:
        gate = _dot(u, wg_ref[:, off:off + c])
        up = _dot(u, wu_ref[:, off:off + c])
        act_ref[:, off:off + c] = (_silu(gate) * up).astype(BF16)
        off += c
    y = x + _dot(act_ref[...], wd_ref[...])
    if final_norm:
        y = _rms(y, refs[7][...], NORM_EPS)
    o_ref[...] = y


def _proj_ffn(h, a, w_proj, k, g, w_gate, w_up, w_down, i, final_g=None):
    n, kdim = a.shape
    tm = min(n, TOKEN_TILE)
    row = lambda w: pl.BlockSpec((tm, w), lambda t: (t, 0))
    final_norm = final_g is not None
    args = [h, a, w_proj, g, w_gate, w_up, w_down] + ([final_g] if final_norm else [])
    return pl.pallas_call(
        functools.partial(_ffn_kernel, final_norm=final_norm),
        grid=(n // tm,),
        in_specs=[row(D_MODEL), row(kdim), _resident_layer(w_proj, k), _resident(g.shape),
                  _resident_layer(w_gate, i), _resident_layer(w_up, i), _resident_layer(w_down, i)]
        + ([_resident(final_g.shape)] if final_norm else []),
        out_specs=row(D_MODEL),
        out_shape=jax.ShapeDtypeStruct((n, D_MODEL), F32),
        scratch_shapes=[pltpu.VMEM((tm, FFN_HIDDEN), BF16)],
        compiler_params=_params("parallel"),
        name="proj_ffn_final" if final_norm else "proj_ffn",
    )(*args)


def _ssm_in_kernel(x_ref, g_ref, w_ref, wdt_ref, z_ref, xbc_ref, dt_ref):
    u = _rms(x_ref[...], g_ref[...], NORM_EPS).astype(BF16)
    z_ref[...] = _silu(_dot(u, w_ref[:, :SSM_INNER])).astype(BF16)
    step = SSM_CONV_DIM // 4
    for c in range(0, SSM_CONV_DIM, step):
        xbc_ref[:, c:c + step] = _dot(u, w_ref[:, SSM_INNER + c:SSM_INNER + c + step]).astype(BF16)
    dt_ref[...] = _dot(u, wdt_ref[...])


def _ssm_in(h, g, w_in, k, wdt):
    n = h.shape[0]
    tm = min(n, TOKEN_TILE)
    row = lambda w: pl.BlockSpec((tm, w), lambda t: (t, 0))
    return pl.pallas_call(
        _ssm_in_kernel,
        grid=(n // tm,),
        in_specs=[row(D_MODEL), _resident(g.shape), _resident_layer(w_in, k), _resident(wdt.shape)],
        out_specs=[row(SSM_INNER), row(SSM_CONV_DIM), row(LANES)],
        out_shape=[jax.ShapeDtypeStruct((n, SSM_INNER), BF16), jax.ShapeDtypeStruct((n, SSM_CONV_DIM), BF16),
                   jax.ShapeDtypeStruct((n, LANES), F32)],
        compiler_params=_params("parallel"),
        name="ssm_in",
    )(h, g, w_in, wdt)


def _split3(x):
    hi = x.astype(BF16)
    r = x - hi.astype(F32)
    mid = r.astype(BF16)
    lo = (r - mid.astype(F32)).astype(BF16)
    return hi, mid, lo


def _ssd_kernel(xbc_ref, z_ref, dt_ref, cw_ref, cb_ref, shift_ref, dtb_ref, alog_ref, dskip_ref, nw_ref,
                tril_ref, expand_ref, y_ref, state_ref, hist_ref, act_ref):
    L = SSM_CHUNK
    n_chunks = xbc_ref.shape[0] // L

    @pl.when(pl.program_id(1) == 0)
    def _():
        state_ref[...] = jnp.zeros_like(state_ref)
        hist_ref[...] = jnp.zeros_like(hist_ref)

    shift = shift_ref[...]
    cw16 = cw_ref[0:SSM_CONV - 1, :].astype(BF16)
    def conv_block(cc, c0):
        cs = slice(c0, c0 + CONV_COLS)
        cur = xbc_ref[cc * L:(cc + 1) * L, cs]
        prev = hist_ref[:, cs] if cc == 0 else xbc_ref[cc * L - CONV_HIST:cc * L, cs]
        ext = jnp.concatenate([prev, cur], axis=0)
        taps = [ext * cw16[SSM_CONV - 1 - d:SSM_CONV - d, cs] for d in range(1, SSM_CONV)]
        acc = (_dot(shift, jnp.concatenate(taps, axis=0))
               + cur.astype(F32) * cw_ref[SSM_CONV - 1:SSM_CONV, cs] + cb_ref[:, cs])
        act_ref[cc * L:(cc + 1) * L, cs] = _silu(acc).astype(BF16)

    conv_cols = range(0, SSM_CONV_DIM, CONV_COLS)
    for c0 in conv_cols:
        conv_block(0, c0)
    rows = lax.broadcasted_iota(jnp.int32, (L, L), 0)
    cols = lax.broadcasted_iota(jnp.int32, (L, L), 1)
    causal = rows >= cols
    first = lax.broadcasted_iota(jnp.int32, (L, LANES), 1) < SSM_HEADDIM
    neg_a = -jnp.exp(alog_ref[...])
    tril = tril_ref[...]

    for cc in range(n_chunks):
        r = slice(cc * L, (cc + 1) * L)
        dtv = dt_ref[r, :] + dtb_ref[...]
        dt = jnp.maximum(dtv, 0.0) + jnp.log1p(jnp.exp(-jnp.abs(dtv)))
        hi, mid, lo = _split3(dt * neg_a)
        cum = _dot(tril, hi) + _dot(tril, mid) + _dot(tril, lo)
        cum_t = cum.T
        dt_t = dt.T
        last_t = cum_t[:, L - 1:L]
        ws16 = (jnp.exp(last_t - cum_t) * dt_t).astype(BF16)
        elast_t = jnp.exp(last_t)
        cum2 = cum * LOG2_E
        row2_t = jnp.where(dt_t > 0.0, cum_t * LOG2_E - jnp.log2(dt_t), jnp.inf)
        col2 = _dot(jnp.concatenate(_split3(cum2), axis=1), expand_ref[...])

        cbs, b_ts, y_ins = [], [], []
        for g in range(SSM_GROUPS):
            b0 = SSM_INNER + g * SSM_STATE
            b16 = act_ref[r, b0:b0 + SSM_STATE]
            c16 = act_ref[r, b0 + SSM_BC_DIM:b0 + SSM_BC_DIM + SSM_STATE]
            cbs.append(_dot_nt(c16, b16))
            b_ts.append(b16.astype(F32).T.astype(BF16))
            y_ins.append(_dot(c16, state_ref[g].astype(BF16)))
            if cc + 1 < n_chunks:
                conv_block(cc + 1, conv_cols[g])

        ws, bss = [], []
        for hd in range(SSM_HEADS):
            g = hd // SSM_HPG
            col = col2[:, hd * L:(hd + 1) * L]
            decay_dt = jnp.exp2(jnp.where(causal, col - row2_t[hd:hd + 1, :], -jnp.inf))
            ws.append((cbs[g] * decay_dt).astype(BF16))
            bss.append(b_ts[g] * ws16[hd:hd + 1, :])

        y_intra, s_new = [], []
        for pp in range(SSM_HEADS // 2):
            ha, hb = 2 * pp, 2 * pp + 1
            xp16 = act_ref[r, pp * LANES:(pp + 1) * LANES]
            zero = jnp.zeros_like(xp16)
            xbd = jnp.concatenate([jnp.where(first, xp16, zero), jnp.where(first, zero, xp16)], axis=0)
            y_intra.append(_dot(jnp.concatenate([ws[ha], ws[hb]], axis=1), xbd))
            s_new.append(_dot(jnp.concatenate([bss[ha], bss[hb]], axis=1), xbd))
        gated = []
        for pp in range(SSM_HEADS // 2):
            g, pr = divmod(pp, SSM_HPG // 2)
            c0 = pp * LANES
            sl = slice(pr * LANES, (pr + 1) * LANES)
            ha, hb = 2 * pp, 2 * pp + 1
            e_col = jnp.where(first, jnp.exp2(col2[:, ha * L:(ha + 1) * L]), jnp.exp2(col2[:, hb * L:(hb + 1) * L]))
            y = (y_intra[pp] + y_ins[g][:, sl] * e_col
                 + dskip_ref[:, c0:c0 + LANES] * act_ref[r, c0:c0 + LANES].astype(F32))
            state_ref[g, :, sl] = (state_ref[g, :, sl] * jnp.where(first, elast_t[ha:ha + 1, :], elast_t[hb:hb + 1, :])
                                   + s_new[pp])
            gated.append(y * z_ref[r, c0:c0 + LANES].astype(F32))

        for g in range(SSM_GROUPS):
            x0 = g * SSM_GROUP_W
            gg = jnp.concatenate(gated[g * (SSM_HPG // 2):(g + 1) * (SSM_HPG // 2)], axis=1)
            ms = jnp.mean(gg * gg, axis=-1, keepdims=True)
            y_ref[r, x0:x0 + SSM_GROUP_W] = (gg * lax.rsqrt(ms + SSM_NORM_EPS)
                                             * nw_ref[:, x0:x0 + SSM_GROUP_W]).astype(y_ref.dtype)

    hist_ref[...] = xbc_ref[n_chunks * L - CONV_HIST:n_chunks * L, :]


def _ssd(xbc, z, dt, conv_w, conv_b, shift, dt_bias, a_log, d_skip, norm_w, tril, expand):
    b, s, _ = xbc.shape
    rows = SSD_STEP_CHUNKS * SSM_CHUNK
    blk = lambda w: pl.BlockSpec((None, rows, w), lambda bi, c: (bi, c, 0))
    consts = [conv_w, conv_b, shift, dt_bias, a_log, d_skip, norm_w, tril, expand]
    return pl.pallas_call(
        _ssd_kernel,
        grid=(b, s // rows),
        in_specs=[blk(SSM_CONV_DIM), blk(SSM_INNER), blk(LANES)] + [_resident(c.shape) for c in consts],
        out_specs=blk(SSM_INNER),
        out_shape=jax.ShapeDtypeStruct((b, s, SSM_INNER), BF16),
        scratch_shapes=[pltpu.VMEM((SSM_GROUPS, SSM_STATE, SSM_GROUP_W), F32),
                        pltpu.VMEM((CONV_HIST, SSM_CONV_DIM), BF16),
                        pltpu.VMEM((rows, SSM_CONV_DIM), BF16)],
        compiler_params=_params("parallel", "arbitrary"),
        name="ssd",
    )(xbc, z, dt, *consts)


def _conv_shift_matrix():
    e = CONV_HIST + SSM_CHUNK
    t = jnp.arange(SSM_CHUNK)[:, None]
    j = jnp.arange(e)[None, :]
    blocks = [(j == t - d + CONV_HIST) for d in range(1, SSM_CONV)]
    return jnp.concatenate(blocks, axis=1).astype(BF16)


def _mla_weights(w_in, w_uq, w_ukv):
    kr = w_in[:, Q_LORA + KV_LORA:]
    w_in2 = jnp.concatenate([w_in[:, :Q_LORA + KV_LORA], kr, kr], axis=1)
    uq = w_uq.reshape(Q_LORA, MLA_HEADS, QK_NOPE + QK_ROPE)
    q_nope = uq[:, :, :QK_NOPE].reshape(Q_LORA, -1)
    q_rope = uq[:, :, QK_NOPE:].reshape(Q_LORA, -1)
    w_uq2 = jnp.concatenate([q_nope, q_rope], axis=1)
    ukv = w_ukv.reshape(KV_LORA, MLA_HEADS, QK_NOPE + V_DIM)
    w_ukv2 = jnp.concatenate([ukv[:, :, :QK_NOPE].reshape(KV_LORA, -1),
                              ukv[:, :, QK_NOPE:].reshape(KV_LORA, -1)], axis=1)
    return w_in2.astype(BF16), w_uq2.astype(BF16), w_ukv2.astype(BF16)


def _pad_lanes(v):
    return jnp.pad(v, (0, LANES - v.shape[0]))[None, :]


def kernel(x, positions, mix_norm, ffn_norm, final_norm, mla_w_in, mla_q_norm, mla_kv_norm, mla_w_uq,
           mla_w_ukv, mla_w_o, ssm_w_in, ssm_conv_w, ssm_conv_b, ssm_dt_bias, ssm_A_log, ssm_D, ssm_norm,
           ssm_w_out, ffn_w_gate, ffn_w_up, ffn_w_down):
    b, s, d = x.shape
    n = b * s
    cos, sin = _rope_tables(positions)
    tril = jnp.tril(jnp.ones((SSM_CHUNK, SSM_CHUNK), BF16))
    expand = jnp.tile(jnp.repeat(jnp.eye(LANES, SSM_HEADS, dtype=BF16), SSM_CHUNK, axis=1), (3, 1))
    shift = _conv_shift_matrix()
    mla_w_o16, ssm_w_in16, ssm_w_out16 = (w.astype(BF16) for w in (mla_w_o, ssm_w_in, ssm_w_out))
    ffn_wg16, ffn_wu16, ffn_wd16 = (w.astype(BF16) for w in (ffn_w_gate, ffn_w_up, ffn_w_down))

    h = x.reshape(n, d)
    for i in range(DEPTH):
        k = i // N_MIXERS
        g_mix = mix_norm[i][None, :]
        if i % N_MIXERS == 0:
            w_in, w_uq, w_ukv = _mla_weights(mla_w_in[k], mla_w_uq[k], mla_w_ukv[k])
            q, kk, v = _mla_proj(h, g_mix, w_in, mla_q_norm[k][None, :], mla_kv_norm[k][None, :],
                                 w_uq, w_ukv, cos, sin)
            a = _attention(q.reshape(b, s, -1), kk.reshape(b, s, -1), v.reshape(b, s, -1))
            w_proj = mla_w_o16
        else:
            wdt = jnp.pad(ssm_w_in[k][:, SSM_INNER + SSM_CONV_DIM:],
                          ((0, 0), (0, LANES - SSM_HEADS))).astype(BF16)
            z, xbc, dt = _ssm_in(h, g_mix, ssm_w_in16, k, wdt)
            a = _ssd(xbc.reshape(b, s, -1), z.reshape(b, s, -1), dt.reshape(b, s, -1), ssm_conv_w[k],
                     ssm_conv_b[k][None, :], shift, _pad_lanes(ssm_dt_bias[k]), _pad_lanes(ssm_A_log[k]),
                     jnp.repeat(ssm_D[k], SSM_HEADDIM)[None, :], ssm_norm[k][None, :], tril, expand)
            w_proj = ssm_w_out16
        h = _proj_ffn(h, a.reshape(n, -1), w_proj, k, ffn_norm[i][None, :], ffn_wg16, ffn_wu16, ffn_wd16, i,
                      final_norm[None, :] if i == DEPTH - 1 else None)
    return h.reshape(b, s, d)
```

```python
import functools
import math

import jax
import jax.numpy as jnp
from jax import lax
from jax.experimental import pallas as pl
from jax.experimental.pallas import tpu as pltpu

F32 = jnp.float32
BF16 = jnp.bfloat16

D_MODEL = 1024
DEPTH = 4
N_MIXERS = 2

MLA_HEADS = 8
Q_LORA = 512
KV_LORA = 256
QK_NOPE = 128
QK_ROPE = 64
V_DIM = 128
ROPE_THETA = 10000.0
ROPE_HALF = QK_ROPE // 2
HEAD_PAIRS = MLA_HEADS // 2
Q_PAIR_W = 2 * QK_NOPE + 2 * QK_ROPE
K_PAIR_W = 2 * QK_NOPE + 4 * QK_ROPE
QK_CAT = QK_NOPE + 2 * QK_ROPE
LOG2_E = math.log2(math.e)
Q_SCALE = (QK_NOPE + QK_ROPE) ** -0.5 * LOG2_E

SSM_INNER = 2 * D_MODEL
SSM_HEADDIM = 64
SSM_HEADS = SSM_INNER // SSM_HEADDIM
SSM_GROUPS = 8
SSM_HPG = SSM_HEADS // SSM_GROUPS
SSM_STATE = 128
SSM_CONV = 4
SSM_CHUNK = 128
SSM_BC_DIM = SSM_GROUPS * SSM_STATE
SSM_CONV_DIM = SSM_INNER + 2 * SSM_BC_DIM
SSM_GROUP_W = SSM_HPG * SSM_HEADDIM
SSM_NORM_EPS = 1e-5

FFN_HIDDEN = 2816
NORM_EPS = 1e-6

LANES = 128
SUBLANES = 8
VMEM_LIMIT_BYTES = 56 * 1024 * 1024

TOKEN_TILE = 512
ATTN_TILE = 512
FFN_CHUNKS = (512, 512, 512, 512, 512, 256)
CONV_HIST = 2 * SUBLANES
CONV_COLS = 512
SSD_STEP_CHUNKS = 4
assert SSM_CONV_DIM // CONV_COLS == SSM_GROUPS


def _params(*sem, flags=None):
    return pltpu.CompilerParams(dimension_semantics=sem, vmem_limit_bytes=VMEM_LIMIT_BYTES, flags=flags)


def _resident(shape):
    return pl.BlockSpec(shape, lambda *_: (0,) * len(shape), pipeline_mode=pl.Buffered(1))


def _resident_layer(stack, k):
    return pl.BlockSpec((None,) + stack.shape[1:], lambda *_: (k, 0, 0), pipeline_mode=pl.Buffered(1))


def _rms(x, g, eps):
    return x * lax.rsqrt(jnp.mean(x * x, axis=-1, keepdims=True) + eps) * g


def _silu(x):
    return x * jax.nn.sigmoid(x)


def _dot(a, b):
    return jnp.dot(a, b, preferred_element_type=F32)


def _dot_nt(a, b):
    return lax.dot_general(a, b, (((1,), (1,)), ((), ())), preferred_element_type=F32)


def _rope_kernel(pos_ref, inv_ref, sgn_ref, cos_ref, sin_ref):
    ang = pos_ref[...].astype(F32) * inv_ref[...]
    cos_ref[...] = jnp.cos(ang)
    sin_ref[...] = jnp.sin(ang) * sgn_ref[...]


def _rope_tables(positions):
    n = positions.size
    tm = min(n, 2048)
    inv = jnp.power(ROPE_THETA, -jnp.arange(ROPE_HALF, dtype=F32) / ROPE_HALF)
    inv = jnp.tile(inv, LANES // ROPE_HALF)[None, :]
    sgn = jnp.tile(jnp.concatenate([-jnp.ones(ROPE_HALF, F32), jnp.ones(ROPE_HALF, F32)]),
                   LANES // QK_ROPE)[None, :]
    row = pl.BlockSpec((tm, LANES), lambda i: (i, 0))
    return pl.pallas_call(
        _rope_kernel,
        grid=(n // tm,),
        in_specs=[pl.BlockSpec((tm, 1), lambda i: (i, 0)), _resident((1, LANES)), _resident((1, LANES))],
        out_specs=[row, row],
        out_shape=[jax.ShapeDtypeStruct((n, LANES), F32)] * 2,
        compiler_params=_params("parallel"),
        name="rope_tables",
    )(positions.reshape(n, 1), inv, sgn)


def _mla_proj_kernel(x_ref, g_ref, win_ref, qn_ref, kvn_ref, wuq_ref, wukv_ref, cos_ref, sin_ref,
                     q_ref, k_ref, v_ref):
    u = _rms(x_ref[...], g_ref[...], NORM_EPS).astype(BF16)
    lat = _dot(u, win_ref[...])
    cos = cos_ref[...]
    sin = sin_ref[...]
    kv0 = Q_LORA
    kr0 = Q_LORA + KV_LORA
    qn = _rms(lat[:, :Q_LORA], qn_ref[...], NORM_EPS).astype(BF16)
    kvn = _rms(lat[:, kv0:kr0], kvn_ref[...], NORM_EPS).astype(BF16)
    q = _dot(qn, wuq_ref[...])
    kv = _dot(kvn, wukv_ref[...])

    lane = lax.broadcasted_iota(jnp.int32, cos.shape, 1)
    low_half = lane % QK_ROPE < ROPE_HALF

    def rope(x):
        swapped = jnp.where(low_half, pltpu.roll(x, LANES - ROPE_HALF, 1), pltpu.roll(x, ROPE_HALF, 1))
        return x * cos + swapped * sin

    kr = rope(lat[:, kr0:kr0 + LANES])
    kr_lo = jnp.where(lane < QK_ROPE, kr, 0.0).astype(BF16)
    kr_hi = jnp.where(lane >= QK_ROPE, kr, 0.0).astype(BF16)

    nope_w = MLA_HEADS * QK_NOPE
    for p in range(HEAD_PAIRS):
        a = 2 * p * QK_NOPE
        b = a + QK_NOPE
        r = nope_w + p * LANES
        qr = rope(q[:, r:r + LANES])
        q0 = p * Q_PAIR_W
        q_ref[:, q0:q0 + LANES] = (q[:, a:a + QK_NOPE] * Q_SCALE).astype(BF16)
        q_ref[:, q0 + LANES:q0 + 2 * LANES] = (qr * Q_SCALE).astype(BF16)
        q_ref[:, q0 + 2 * LANES:q0 + 3 * LANES] = (q[:, b:b + QK_NOPE] * Q_SCALE).astype(BF16)
        k0 = p * K_PAIR_W
        k_ref[:, k0:k0 + LANES] = kv[:, a:a + QK_NOPE].astype(BF16)
        k_ref[:, k0 + LANES:k0 + 2 * LANES] = kr_lo
        k_ref[:, k0 + 2 * LANES:k0 + 3 * LANES] = kr_hi
        k_ref[:, k0 + 3 * LANES:k0 + 4 * LANES] = kv[:, b:b + QK_NOPE].astype(BF16)
    v_ref[...] = kv[:, nope_w:].astype(BF16)


def _mla_proj(h, g, w_in, q_norm, kv_norm, w_uq, w_ukv, cos, sin):
    n = h.shape[0]
    tm = min(n, TOKEN_TILE)
    qw, kw, vw = HEAD_PAIRS * Q_PAIR_W, HEAD_PAIRS * K_PAIR_W, MLA_HEADS * V_DIM
    row = lambda w: pl.BlockSpec((tm, w), lambda i: (i, 0))
    return pl.pallas_call(
        _mla_proj_kernel,
        grid=(n // tm,),
        in_specs=[row(D_MODEL), _resident(g.shape), _resident(w_in.shape), _resident(q_norm.shape),
                  _resident(kv_norm.shape), _resident(w_uq.shape), _resident(w_ukv.shape),
                  row(LANES), row(LANES)],
        out_specs=[row(qw), row(kw), row(vw)],
        out_shape=[jax.ShapeDtypeStruct((n, qw), BF16), jax.ShapeDtypeStruct((n, kw), BF16),
                   jax.ShapeDtypeStruct((n, vw), BF16)],
        compiler_params=_params("parallel"),
        name="mla_proj",
    )(h, g, w_in, q_norm, kv_norm, w_uq, w_ukv, cos, sin)


def _attn_kernel(q_ref, k_ref, v_ref, o_ref):
    t = ATTN_TILE
    rows = lax.broadcasted_iota(jnp.int32, (t, t), 0)
    cols = lax.broadcasted_iota(jnp.int32, (t, t), 1)
    causal = cols <= rows

    units = [(i * t, hh) for i in range(q_ref.shape[0] // t) for hh in range(2)]
    kcol = lambda hh: slice(hh * QK_CAT, (hh + 1) * QK_CAT)
    vcol = lambda hh: slice(hh * V_DIM, (hh + 1) * V_DIM)

    scores = [_dot_nt(q_ref[r0:r0 + t, hh * LANES:hh * LANES + QK_CAT], k_ref[0:r0 + t, kcol(hh)])
              for r0, hh in units]
    probs = []
    for (r0, hh), s in zip(units, scores):
        s_d = jnp.where(causal, s[:, r0:], -jnp.inf)
        m = jnp.max(s_d, axis=-1, keepdims=True)
        if r0:
            m = jnp.maximum(m, jnp.max(s[:, :r0], axis=-1, keepdims=True))
        p = jnp.exp2(s_d - m).astype(BF16)
        if r0:
            p = jnp.concatenate([jnp.exp2(s[:, :r0] - m).astype(BF16), p], axis=1)
        probs.append(p)
    ones = jnp.ones((q_ref.shape[0], V_DIM), BF16)
    for (r0, hh), p in zip(units, probs):
        n = r0 + t
        acc = _dot(p, jnp.concatenate([v_ref[0:n, vcol(hh)], ones[0:n]], axis=1))
        o_ref[r0:r0 + t, vcol(hh)] = (acc[:, :V_DIM] / acc[:, V_DIM:]).astype(BF16)


def _attention(q, k, v):
    b, s, _ = q.shape
    blk = lambda w: pl.BlockSpec((None, s, w), lambda bi, p: (bi, 0, p))
    return pl.pallas_call(
        _attn_kernel,
        grid=(b, HEAD_PAIRS),
        in_specs=[blk(Q_PAIR_W), blk(K_PAIR_W), blk(2 * V_DIM)],
        out_specs=blk(2 * V_DIM),
        out_shape=jax.ShapeDtypeStruct((b, s, MLA_HEADS * V_DIM), BF16),
        compiler_params=_params("parallel", "parallel"),
        name="mla_attention",
    )(q, k, v)


def _ffn_kernel(*refs, final_norm):
    h_ref, a_ref, wp_ref, g_ref, wg_ref, wu_ref, wd_ref = refs[:7]
    o_ref, act_ref = refs[-2:]
    x = h_ref[...] + _dot(a_ref[...], wp_ref[...])
    u = _rms(x, g_ref[...], NORM_EPS).astype(BF16)
    off = 0
    for c in FFN_CHUNKS:
        gate = _dot(u, wg_ref[:, off:off + c])
        up = _dot(u, wu_ref[:, off:off + c])
        act_ref[:, off:off + c] = (_silu(gate) * up).astype(BF16)
        off += c
    y = x + _dot(act_ref[...], wd_ref[...])
    if final_norm:
        y = _rms(y, refs[7][...], NORM_EPS)
    o_ref[...] = y


def _proj_ffn(h, a, w_proj, k, g, w_gate, w_up, w_down, i, final_g=None):
    n, kdim = a.shape
    tm = min(n, TOKEN_TILE)
    row = lambda w: pl.BlockSpec((tm, w), lambda t: (t, 0))
    final_norm = final_g is not None
    args = [h, a, w_proj, g, w_gate, w_up, w_down] + ([final_g] if final_norm else [])
    return pl.pallas_call(
        functools.partial(_ffn_kernel, final_norm=final_norm),
        grid=(n // tm,),
        in_specs=[row(D_MODEL), row(kdim), _resident_layer(w_proj, k), _resident(g.shape),
                  _resident_layer(w_gate, i), _resident_layer(w_up, i), _resident_layer(w_down, i)]
        + ([_resident(final_g.shape)] if final_norm else []),
        out_specs=row(D_MODEL),
        out_shape=jax.ShapeDtypeStruct((n, D_MODEL), F32),
        scratch_shapes=[pltpu.VMEM((tm, FFN_HIDDEN), BF16)],
        compiler_params=_params("parallel"),
        name="proj_ffn_final" if final_norm else "proj_ffn",
    )(*args)


def _ssm_in_kernel(x_ref, g_ref, w_ref, wdt_ref, z_ref, xbc_ref, dt_ref):
    u = _rms(x_ref[...], g_ref[...], NORM_EPS).astype(BF16)
    z_ref[...] = _silu(_dot(u, w_ref[:, :SSM_INNER])).astype(BF16)
    step = SSM_CONV_DIM // 4
    for c in range(0, SSM_CONV_DIM, step):
        xbc_ref[:, c:c + step] = _dot(u, w_ref[:, SSM_INNER + c:SSM_INNER + c + step]).astype(BF16)
    dt_ref[...] = _dot(u, wdt_ref[...])


def _ssm_in(h, g, w_in, k, wdt):
    n = h.shape[0]
    tm = min(n, TOKEN_TILE)
    row = lambda w: pl.BlockSpec((tm, w), lambda t: (t, 0))
    return pl.pallas_call(
        _ssm_in_kernel,
        grid=(n // tm,),
        in_specs=[row(D_MODEL), _resident(g.shape), _resident_layer(w_in, k), _resident(wdt.shape)],
        out_specs=[row(SSM_INNER), row(SSM_CONV_DIM), row(LANES)],
        out_shape=[jax.ShapeDtypeStruct((n, SSM_INNER), BF16), jax.ShapeDtypeStruct((n, SSM_CONV_DIM), BF16),
                   jax.ShapeDtypeStruct((n, LANES), F32)],
        compiler_params=_params("parallel"),
        name="ssm_in",
    )(h, g, w_in, wdt)


def _split3(x):
    hi = x.astype(BF16)
    r = x - hi.astype(F32)
    mid = r.astype(BF16)
    lo = (r - mid.astype(F32)).astype(BF16)
    return hi, mid, lo


def _ssd_kernel(xbc_ref, z_ref, dt_ref, cw_ref, cb_ref, shift_ref, dtb_ref, alog_ref, dskip_ref, nw_ref,
                tril_ref, expand_ref, y_ref, state_ref, hist_ref, act_ref):
    L = SSM_CHUNK
    n_chunks = xbc_ref.shape[0] // L

    @pl.when(pl.program_id(1) == 0)
    def _():
        state_ref[...] = jnp.zeros_like(state_ref)
        hist_ref[...] = jnp.zeros_like(hist_ref)

    shift = shift_ref[...]
    cw16 = cw_ref[0:SSM_CONV - 1, :].astype(BF16)
    def conv_block(cc, c0):
        cs = slice(c0, c0 + CONV_COLS)
        cur = xbc_ref[cc * L:(cc + 1) * L, cs]
        prev = hist_ref[:, cs] if cc == 0 else xbc_ref[cc * L - CONV_HIST:cc * L, cs]
        ext = jnp.concatenate([prev, cur], axis=0)
        taps = [ext * cw16[SSM_CONV - 1 - d:SSM_CONV - d, cs] for d in range(1, SSM_CONV)]
        acc = (_dot(shift, jnp.concatenate(taps, axis=0))
               + cur.astype(F32) * cw_ref[SSM_CONV - 1:SSM_CONV, cs] + cb_ref[:, cs])
        act_ref[cc * L:(cc + 1) * L, cs] = _silu(acc).astype(BF16)

    conv_cols = range(0, SSM_CONV_DIM, CONV_COLS)
    for c0 in conv_cols:
        conv_block(0, c0)
    rows = lax.broadcasted_iota(jnp.int32, (L, L), 0)
    cols = lax.broadcasted_iota(jnp.int32, (L, L), 1)
    causal = rows >= cols
    first = lax.broadcasted_iota(jnp.int32, (L, LANES), 1) < SSM_HEADDIM
    neg_a = -jnp.exp(alog_ref[...])
    tril = tril_ref[...]

    for cc in range(n_chunks):
        r = slice(cc * L, (cc + 1) * L)
        dtv = dt_ref[r, :] + dtb_ref[...]
        dt = jnp.maximum(dtv, 0.0) + jnp.log1p(jnp.exp(-jnp.abs(dtv)))
        hi, mid, lo = _split3(dt * neg_a)
        cum = _dot(tril, hi) + _dot(tril, mid) + _dot(tril, lo)
        cum_t = cum.T
        dt_t = dt.T
        last_t = cum_t[:, L - 1:L]
        ws16 = (jnp.exp(last_t - cum_t) * dt_t).astype(BF16)
        elast_t = jnp.exp(last_t)
        cum2 = cum * LOG2_E
        row2_t = jnp.where(dt_t > 0.0, cum_t * LOG2_E - jnp.log2(dt_t), jnp.inf)
        col2 = _dot(jnp.concatenate(_split3(cum2), axis=1), expand_ref[...])

        cbs, b_ts, y_ins = [], [], []
        for g in range(SSM_GROUPS):
            b0 = SSM_INNER + g * SSM_STATE
            b16 = act_ref[r, b0:b0 + SSM_STATE]
            c16 = act_ref[r, b0 + SSM_BC_DIM:b0 + SSM_BC_DIM + SSM_STATE]
            cbs.append(_dot_nt(c16, b16))
            b_ts.append(b16.astype(F32).T.astype(BF16))
            y_ins.append(_dot(c16, state_ref[g].astype(BF16)))
            if cc + 1 < n_chunks:
                conv_block(cc + 1, conv_cols[g])

        ws, bss = [], []
        for hd in range(SSM_HEADS):
            g = hd // SSM_HPG
            col = col2[:, hd * L:(hd + 1) * L]
            decay_dt = jnp.exp2(jnp.where(causal, col - row2_t[hd:hd + 1, :], -jnp.inf))
            ws.append((cbs[g] * decay_dt).astype(BF16))
            bss.append(b_ts[g] * ws16[hd:hd + 1, :])

        y_intra, s_new = [], []
        for pp in range(SSM_HEADS // 2):
            ha, hb = 2 * pp, 2 * pp + 1
            xp16 = act_ref[r, pp * LANES:(pp + 1) * LANES]
            zero = jnp.zeros_like(xp16)
            xbd = jnp.concatenate([jnp.where(first, xp16, zero), jnp.where(first, zero, xp16)], axis=0)
            y_intra.append(_dot(jnp.concatenate([ws[ha], ws[hb]], axis=1), xbd))
            s_new.append(_dot(jnp.concatenate([bss[ha], bss[hb]], axis=1), xbd))
        gated = []
        for pp in range(SSM_HEADS // 2):
            g, pr = divmod(pp, SSM_HPG // 2)
            c0 = pp * LANES
            sl = slice(pr * LANES, (pr + 1) * LANES)
            ha, hb = 2 * pp, 2 * pp + 1
            e_col = jnp.where(first, jnp.exp2(col2[:, ha * L:(ha + 1) * L]), jnp.exp2(col2[:, hb * L:(hb + 1) * L]))
            y = (y_intra[pp] + y_ins[g][:, sl] * e_col
                 + dskip_ref[:, c0:c0 + LANES] * act_ref[r, c0:c0 + LANES].astype(F32))
            state_ref[g, :, sl] = (state_ref[g, :, sl] * jnp.where(first, elast_t[ha:ha + 1, :], elast_t[hb:hb + 1, :])
                                   + s_new[pp])
            gated.append(y * z_ref[r, c0:c0 + LANES].astype(F32))

        for g in range(SSM_GROUPS):
            x0 = g * SSM_GROUP_W
            gg = jnp.concatenate(gated[g * (SSM_HPG // 2):(g + 1) * (SSM_HPG // 2)], axis=1)
            ms = jnp.mean(gg * gg, axis=-1, keepdims=True)
            y_ref[r, x0:x0 + SSM_GROUP_W] = (gg * lax.rsqrt(ms + SSM_NORM_EPS)
                                             * nw_ref[:, x0:x0 + SSM_GROUP_W]).astype(y_ref.dtype)

    hist_ref[...] = xbc_ref[n_chunks * L - CONV_HIST:n_chunks * L, :]


def _ssd(xbc, z, dt, conv_w, conv_b, shift, dt_bias, a_log, d_skip, norm_w, tril, expand):
    b, s, _ = xbc.shape
    rows = SSD_STEP_CHUNKS * SSM_CHUNK
    blk = lambda w: pl.BlockSpec((None, rows, w), lambda bi, c: (bi, c, 0))
    consts = [conv_w, conv_b, shift, dt_bias, a_log, d_skip, norm_w, tril, expand]
    return pl.pallas_call(
        _ssd_kernel,
        grid=(b, s // rows),
        in_specs=[blk(SSM_CONV_DIM), blk(SSM_INNER), blk(LANES)] + [_resident(c.shape) for c in consts],
        out_specs=blk(SSM_INNER),
        out_shape=jax.ShapeDtypeStruct((b, s, SSM_INNER), BF16),
        scratch_shapes=[pltpu.VMEM((SSM_GROUPS, SSM_STATE, SSM_GROUP_W), F32),
                        pltpu.VMEM((CONV_HIST, SSM_CONV_DIM), BF16),
                        pltpu.VMEM((rows, SSM_CONV_DIM), BF16)],
        compiler_params=_params("parallel", "arbitrary"),
        name="ssd",
    )(xbc, z, dt, *consts)


def _conv_shift_matrix():
    e = CONV_HIST + SSM_CHUNK
    t = jnp.arange(SSM_CHUNK)[:, None]
    j = jnp.arange(e)[None, :]
    blocks = [(j == t - d + CONV_HIST) for d in range(1, SSM_CONV)]
    return jnp.concatenate(blocks, axis=1).astype(BF16)


def _mla_weights(w_in, w_uq, w_ukv):
    kr = w_in[:, Q_LORA + KV_LORA:]
    w_in2 = jnp.concatenate([w_in[:, :Q_LORA + KV_LORA], kr, kr], axis=1)
    uq = w_uq.reshape(Q_LORA, MLA_HEADS, QK_NOPE + QK_ROPE)
    q_nope = uq[:, :, :QK_NOPE].reshape(Q_LORA, -1)
    q_rope = uq[:, :, QK_NOPE:].reshape(Q_LORA, -1)
    w_uq2 = jnp.concatenate([q_nope, q_rope], axis=1)
    ukv = w_ukv.reshape(KV_LORA, MLA_HEADS, QK_NOPE + V_DIM)
    w_ukv2 = jnp.concatenate([ukv[:, :, :QK_NOPE].reshape(KV_LORA, -1),
                              ukv[:, :, QK_NOPE:].reshape(KV_LORA, -1)], axis=1)
    return w_in2.astype(BF16), w_uq2.astype(BF16), w_ukv2.astype(BF16)


def _pad_lanes(v):
    return jnp.pad(v, (0, LANES - v.shape[0]))[None, :]


def kernel(x, positions, mix_norm, ffn_norm, final_norm, mla_w_in, mla_q_norm, mla_kv_norm, mla_w_uq,
           mla_w_ukv, mla_w_o, ssm_w_in, ssm_conv_w, ssm_conv_b, ssm_dt_bias, ssm_A_log, ssm_D, ssm_norm,
           ssm_w_out, ffn_w_gate, ffn_w_up, ffn_w_down):
    b, s, d = x.shape
    n = b * s
    cos, sin = _rope_tables(positions)
    tril = jnp.tril(jnp.ones((SSM_CHUNK, SSM_CHUNK), BF16))
    expand = jnp.tile(jnp.repeat(jnp.eye(LANES, SSM_HEADS, dtype=BF16), SSM_CHUNK, axis=1), (3, 1))
    shift = _conv_shift_matrix()
    mla_w_o16, ssm_w_in16, ssm_w_out16 = (w.astype(BF16) for w in (mla_w_o, ssm_w_in, ssm_w_out))
    ffn_wg16, ffn_wu16, ffn_wd16 = (w.astype(BF16) for w in (ffn_w_gate, ffn_w_up, ffn_w_down))

    h = x.reshape(n, d)
    for i in range(DEPTH):
        k = i // N_MIXERS
        g_mix = mix_norm[i][None, :]
        if i % N_MIXERS == 0:
            w_in, w_uq, w_ukv = _mla_weights(mla_w_in[k], mla_w_uq[k], mla_w_ukv[k])
            q, kk, v = _mla_proj(h, g_mix, w_in, mla_q_norm[k][None, :], mla_kv_norm[k][None, :],
                                 w_uq, w_ukv, cos, sin)
            a = _attention(q.reshape(b, s, -1), kk.reshape(b, s, -1), v.reshape(b, s, -1))
            w_proj = mla_w_o16
        else:
            wdt = jnp.pad(ssm_w_in[k][:, SSM_INNER + SSM_CONV_DIM:],
                          ((0, 0), (0, LANES - SSM_HEADS))).astype(BF16)
            z, xbc, dt = _ssm_in(h, g_mix, ssm_w_in16, k, wdt)
            a = _ssd(xbc.reshape(b, s, -1), z.reshape(b, s, -1), dt.reshape(b, s, -1), ssm_conv_w[k],
                     ssm_conv_b[k][None, :], shift, _pad_lanes(ssm_dt_bias[k]), _pad_lanes(ssm_A_log[k]),
                     jnp.repeat(ssm_D[k], SSM_HEADDIM)[None, :], ssm_norm[k][None, :], tril, expand)
            w_proj = ssm_w_out16
        h = _proj_ffn(h, a.reshape(n, -1), w_proj, k, ffn_norm[i][None, :], ffn_wg16, ffn_wu16, ffn_wd16, i,
                      final_norm[None, :] if i == DEPTH - 1 else None)
    return h.reshape(b, s, d)
```
